```python
import math
import jax, jax.numpy as jnp
from jax import lax
import numpy as np

D_MODEL = 1024
BATCH = 4
SEQ = 8192
DEPTH = 2

S5_GROUP = 16
S5_GROUPS = 24
S5_WIDTH = S5_GROUP * S5_GROUPS
S5_STATE = 64
M2_HEADDIM = 64
M2_INNER = 512
M2_HEADS = M2_INNER // M2_HEADDIM
M2_GROUPS = 2
M2_STATE = 128
M2_CONV = 4
M2_CHUNK = 128
M2_CONV_DIM = M2_INNER + 2 * M2_GROUPS * M2_STATE
MLA_HEADS = 8
MLA_Q_RANK = 256
MLA_KV_RANK = 128
MLA_NOPE = 64
MLA_ROPE = 32
MLA_V = 64
MLA_WIDTH = MLA_HEADS * MLA_V
ATTN_BLOCK = 128
ROPE_THETA = 10000.0
PEER_HEADS = 8
PEER_NKEYS = 128
PEER_EXPERTS = PEER_NKEYS * PEER_NKEYS
PEER_KEY_DIM = 256
PEER_TOPK = 16
PEER_TOKEN_BLOCK = 128
N_BRANCHES = 3
IN_SPLITS = (S5_WIDTH, M2_INNER, M2_CONV_DIM, M2_HEADS, MLA_Q_RANK, MLA_KV_RANK, MLA_ROPE, N_BRANCHES * D_MODEL)
IN_WIDTH = sum(IN_SPLITS)
DN_ALPHA = (2 * DEPTH) ** 0.25
DN_BETA = (8 * DEPTH) ** -0.25
LN_EPS = 1e-5
RMS_EPS = 1e-6

kernel_name = "hybrid_s5_ssd_mla_peer_deepnorm"

F32 = jnp.float32


def layer_norm(x, g, b):
    xf = x.astype(F32)
    mu = jnp.mean(xf, -1, keepdims=True)
    var = jnp.mean(jnp.square(xf - mu), -1, keepdims=True)
    return ((xf - mu) * lax.rsqrt(var + LN_EPS) * g.astype(F32) + b.astype(F32)).astype(x.dtype)


def rms_norm(x, g):
    xf = x.astype(F32)
    y = xf * lax.rsqrt(jnp.mean(jnp.square(xf), -1, keepdims=True) + RMS_EPS) * g.astype(F32)
    return y.astype(x.dtype)


def _cplx_scan_op(e1, e2):
    ar1, ai1, br1, bi1 = e1
    ar2, ai2, br2, bi2 = e2
    return (ar1 * ar2 - ai1 * ai2,
            ar1 * ai2 + ai1 * ar2,
            ar2 * br1 - ai2 * bi1 + br2,
            ar2 * bi1 + ai2 * br1 + bi2)


def s5_mixer(u, a_re, a_im, log_dt, b_re, b_im, c_re, c_im, d, w_glu):
    bsz, seq, _ = u.shape
    uf = u.astype(F32)
    ug = uf.reshape(bsz, seq, S5_GROUPS, S5_GROUP)
    lr, li = a_re.astype(F32), a_im.astype(F32)
    dt = jnp.exp(log_dt.astype(F32))[:, None]
    mag = jnp.exp(lr * dt)
    abar_re = mag * jnp.cos(li * dt)
    abar_im = mag * jnp.sin(li * dt)
    nr, ni = abar_re - 1.0, abar_im
    den = lr * lr + li * li
    coef_re = ((nr * lr + ni * li) / den)[..., None]
    coef_im = ((ni * lr - nr * li) / den)[..., None]
    br, bi = b_re.astype(F32), b_im.astype(F32)
    bbar_re = coef_re * br - coef_im * bi
    bbar_im = coef_re * bi + coef_im * br
    bu_re = jnp.einsum('bsgh,gph->bsgp', ug, bbar_re)
    bu_im = jnp.einsum('bsgh,gph->bsgp', ug, bbar_im)
    ar = jnp.broadcast_to(abar_re, bu_re.shape)
    ai = jnp.broadcast_to(abar_im, bu_re.shape)
    _, _, xr, xi = lax.associative_scan(_cplx_scan_op, (ar, ai, bu_re, bu_im), axis=1)
    y = (jnp.einsum('bsgp,ghp->bsgh', xr, c_re.astype(F32))
         - jnp.einsum('bsgp,ghp->bsgh', xi, c_im.astype(F32)))
    y = y.reshape(bsz, seq, S5_WIDTH) + d.astype(F32) * uf
    z = jax.nn.gelu(y)
    out = z * jax.nn.sigmoid(z @ w_glu.astype(F32))
    return out.astype(u.dtype)


def causal_depthwise_conv(x, w, b):
    k = w.shape[0]
    y = lax.conv_general_dilated(x, w.astype(x.dtype)[:, None, :], window_strides=(1,),
                                 padding=((k - 1, 0),), dimension_numbers=('NWC', 'WIO', 'NWC'),
                                 feature_group_count=x.shape[-1])
    return y + b.astype(x.dtype)


def ssd_chunked(x, a, b, c):
    bsz, seq, h, p = x.shape
    n = b.shape[-1]
    L = M2_CHUNK
    nc = seq // L
    x = x.reshape(bsz, nc, L, h, p)
    b = b.reshape(bsz, nc, L, h, n)
    c = c.reshape(bsz, nc, L, h, n)
    a_cum = jnp.cumsum(a.reshape(bsz, nc, L, h).transpose(0, 3, 1, 2), axis=-1)
    causal = jnp.tril(jnp.ones((L, L), dtype=bool))
    seg = a_cum[..., :, None] - a_cum[..., None, :]
    decay = jnp.exp(jnp.where(causal, seg, -jnp.inf))
    scores = jnp.einsum('bclhn,bcshn->bhcls', c, b) * decay
    y_diag = jnp.einsum('bhcls,bcshp->bclhp', scores, x)
    decay_states = jnp.exp(a_cum[..., -1:] - a_cum)
    states = jnp.einsum('bclhn,bhcl,bclhp->bchpn', b, decay_states, x)
    chunk_decay = jnp.exp(a_cum[..., -1])

    def step(carry, inp):
        st, dec = inp
        return carry * dec[..., None, None] + st, carry

    init = jnp.zeros((bsz, h, p, n), F32)
    _, prev = lax.scan(step, init, (states.transpose(1, 0, 2, 3, 4), chunk_decay.transpose(2, 0, 1)))
    prev = prev.transpose(1, 0, 2, 3, 4)
    y_off = jnp.einsum('bclhn,bchpn,bhcl->bclhp', c, prev, jnp.exp(a_cum))
    return (y_diag + y_off).reshape(bsz, seq, h, p)


def mamba2_mixer(z, xbc, dt_raw, conv_w, conv_b, dt_bias, a_log, d, norm_w):
    bsz, seq, _ = z.shape
    xbc = jax.nn.silu(causal_depthwise_conv(xbc, conv_w, conv_b)).astype(F32)
    xs, bm, cm = jnp.split(xbc, [M2_INNER, M2_INNER + M2_GROUPS * M2_STATE], axis=-1)
    xs = xs.reshape(bsz, seq, M2_HEADS, M2_HEADDIM)
    rep = M2_HEADS // M2_GROUPS
    bm = jnp.repeat(bm.reshape(bsz, seq, M2_GROUPS, M2_STATE), rep, axis=2)
    cm = jnp.repeat(cm.reshape(bsz, seq, M2_GROUPS, M2_STATE), rep, axis=2)
    dt = jax.nn.softplus(dt_raw.astype(F32) + dt_bias.astype(F32))
    a = -jnp.exp(a_log.astype(F32))
    y = ssd_chunked(xs * dt[..., None], a * dt, bm, cm)
    y = y + d.astype(F32)[:, None] * xs
    y = y.reshape(bsz, seq, M2_INNER) * jax.nn.silu(z.astype(F32))
    y = y.reshape(bsz, seq, M2_GROUPS, M2_INNER // M2_GROUPS)
    y = y * lax.rsqrt(jnp.mean(y * y, -1, keepdims=True) + RMS_EPS)
    y = y.reshape(bsz, seq, M2_INNER) * norm_w.astype(F32)
    return y.astype(z.dtype)


def rope(x, cos, sin):
    x1, x2 = jnp.split(x, 2, axis=-1)
    return jnp.concatenate([x1 * cos - x2 * sin, x2 * cos + x1 * sin], axis=-1)


def mla_mixer(c_q, c_kv, k_rope, positions, q_norm, w_uq, kv_norm, w_ukv):
    bsz, seq, _ = c_q.shape
    inv_freq = 1.0 / (ROPE_THETA ** (jnp.arange(0, MLA_ROPE, 2, dtype=F32) / MLA_ROPE))
    ang = positions.astype(F32)[..., None] * inv_freq
    cos, sin = jnp.cos(ang), jnp.sin(ang)
    q = (rms_norm(c_q, q_norm) @ w_uq).astype(F32).reshape(bsz, seq, MLA_HEADS, MLA_NOPE + MLA_ROPE)
    scale = (MLA_NOPE + MLA_ROPE) ** -0.5
    q_nope = q[..., :MLA_NOPE] * scale
    q_rope = rope(q[..., MLA_NOPE:], cos[:, :, None], sin[:, :, None]) * scale
    kv = (rms_norm(c_kv, kv_norm) @ w_ukv).astype(F32).reshape(bsz, seq, MLA_HEADS, MLA_NOPE + MLA_V)
    k_nope, v = kv[..., :MLA_NOPE], kv[..., MLA_NOPE:]
    k_r = rope(k_rope.astype(F32), cos, sin)
    nb = seq // ATTN_BLOCK
    qn_b = q_nope.reshape(bsz, nb, ATTN_BLOCK, MLA_HEADS, MLA_NOPE).transpose(1, 0, 2, 3, 4)
    qr_b = q_rope.reshape(bsz, nb, ATTN_BLOCK, MLA_HEADS, MLA_ROPE).transpose(1, 0, 2, 3, 4)
    key_pos = jnp.arange(seq)

    def attend(blk):
        qn, qr, start = blk
        s = jnp.einsum('bqhd,bkhd->bhqk', qn, k_nope) + jnp.einsum('bqhr,bkr->bhqk', qr, k_r)
        q_pos = start + jnp.arange(ATTN_BLOCK)
        s = jnp.where(key_pos[None, :] <= q_pos[:, None], s, -jnp.inf)
        p = jax.nn.softmax(s, axis=-1)
        return jnp.einsum('bhqk,bkhd->bqhd', p, v)

    out = lax.map(attend, (qn_b, qr_b, jnp.arange(nb) * ATTN_BLOCK))
    return out.transpose(1, 0, 2, 3, 4).reshape(bsz, seq, MLA_WIDTH).astype(c_q.dtype)


def peer_ffn(x, w_q, subkeys, u_tab, v_tab):
    bsz, seq, dm = x.shape
    t = bsz * seq
    xt = x.reshape(t, dm)
    q = (xt @ w_q).astype(F32).reshape(t, PEER_HEADS, 2, PEER_KEY_DIM // 2)
    sub = jnp.einsum('thjd,jkd->thjk', q, subkeys.astype(F32))
    s_top, i_top = lax.top_k(sub, PEER_TOPK)
    cand_s = (s_top[:, :, 0, :, None] + s_top[:, :, 1, None, :]).reshape(t, PEER_HEADS, PEER_TOPK * PEER_TOPK)
    cand_i = (i_top[:, :, 0, :, None] * PEER_NKEYS + i_top[:, :, 1, None, :]).reshape(t, PEER_HEADS, PEER_TOPK * PEER_TOPK)
    best_s, best_j = lax.top_k(cand_s, PEER_TOPK)
    idx = jnp.take_along_axis(cand_i, best_j, axis=-1)
    gate = jax.nn.softmax(best_s, axis=-1)
    nblk = t // PEER_TOKEN_BLOCK
    sel = PEER_HEADS * PEER_TOPK
    idx = idx.reshape(nblk, PEER_TOKEN_BLOCK, sel)
    gate = gate.reshape(nblk, PEER_TOKEN_BLOCK, sel)
    xb = xt.reshape(nblk, PEER_TOKEN_BLOCK, dm)

    def experts(blk):
        xi, ii, gi = blk
        u = jnp.take(u_tab, ii, axis=0).astype(F32)
        hdn = jnp.einsum('tkd,td->tk', u, xi.astype(F32))
        act = gi * jax.nn.gelu(hdn)
        v = jnp.take(v_tab, ii, axis=0).astype(F32)
        return jnp.einsum('tk,tkd->td', act, v)

    out = lax.map(experts, (xb, idx, gate))
    return out.reshape(bsz, seq, dm).astype(x.dtype)


def setup_inputs(seed: int = 0) -> dict:
    key = jax.random.key(seed)
    k = jax.random.split(key, 40)
    L = DEPTH

    def nrm(i, shape, scale):
        return jax.random.normal(k[i], shape, F32) * scale

    def gain(i, shape):
        return 1.0 + 0.02 * jax.random.normal(k[i], shape, F32)

    x = nrm(0, (BATCH, SEQ, D_MODEL), 1.0)
    offs = jax.random.randint(k[1], (BATCH, 1), 0, 1024, dtype=jnp.int32)
    positions = offs + jnp.arange(SEQ, dtype=jnp.int32)[None, :]
    ln_in_g = gain(2, (D_MODEL,))
    ln_in_b = nrm(3, (D_MODEL,), 0.02)
    w_in = nrm(4, (L, D_MODEL, IN_WIDTH), D_MODEL ** -0.5)
    s5_a_re = -0.5 + nrm(5, (L, S5_GROUPS, S5_STATE), 0.01)
    s5_a_im = jnp.broadcast_to(math.pi * jnp.arange(S5_STATE, dtype=F32), (L, S5_GROUPS, S5_STATE))
    s5_log_dt = jax.random.uniform(k[6], (L, S5_GROUPS), F32, math.log(1e-3), math.log(1e-1))
    s5_b_re = nrm(7, (L, S5_GROUPS, S5_STATE, S5_GROUP), (2 * S5_GROUP) ** -0.5)
    s5_b_im = nrm(8, (L, S5_GROUPS, S5_STATE, S5_GROUP), (2 * S5_GROUP) ** -0.5)
    s5_c_re = nrm(9, (L, S5_GROUPS, S5_GROUP, S5_STATE), (2 * S5_STATE) ** -0.5)
    s5_c_im = nrm(10, (L, S5_GROUPS, S5_GROUP, S5_STATE), (2 * S5_STATE) ** -0.5)
    s5_d = nrm(11, (L, S5_WIDTH), 1.0)
    s5_w_glu = nrm(12, (L, S5_WIDTH, S5_WIDTH), S5_WIDTH ** -0.5)
    m2_conv_w = nrm(13, (L, M2_CONV, M2_CONV_DIM), M2_CONV ** -0.5)
    m2_conv_b = nrm(14, (L, M2_CONV_DIM), 0.02)
    dt0 = jnp.exp(jax.random.uniform(k[15], (L, M2_HEADS), F32, math.log(1e-3), math.log(1e-1)))
    m2_dt_bias = dt0 + jnp.log(-jnp.expm1(-dt0))
    m2_a_log = jnp.log(jax.random.uniform(k[16], (L, M2_HEADS), F32, 1.0, 16.0))
    m2_d = gain(17, (L, M2_HEADS))
    m2_norm_w = gain(18, (L, M2_INNER))
    mla_q_norm = gain(19, (L, MLA_Q_RANK))
    mla_w_uq = nrm(20, (L, MLA_Q_RANK, MLA_HEADS * (MLA_NOPE + MLA_ROPE)), MLA_Q_RANK ** -0.5)
    mla_kv_norm = gain(21, (L, MLA_KV_RANK))
    mla_w_ukv = nrm(22, (L, MLA_KV_RANK, MLA_HEADS * (MLA_NOPE + MLA_V)), MLA_KV_RANK ** -0.5)
    w_br_a = nrm(23, (L, S5_WIDTH, D_MODEL), DN_BETA * S5_WIDTH ** -0.5)
    w_br_b = nrm(24, (L, M2_INNER, D_MODEL), DN_BETA * M2_INNER ** -0.5)
    w_br_c = nrm(25, (L, MLA_WIDTH, D_MODEL), DN_BETA * MLA_WIDTH ** -0.5)
    w_out = nrm(26, (L, D_MODEL, D_MODEL), DN_BETA * D_MODEL ** -0.5)
    ln1_g = gain(27, (L, D_MODEL))
    ln1_b = nrm(28, (L, D_MODEL), 0.02)
    peer_w_q = nrm(29, (L, D_MODEL, PEER_HEADS * PEER_KEY_DIM), D_MODEL ** -0.5)
    peer_subkeys = nrm(30, (L, 2, PEER_NKEYS, PEER_KEY_DIM // 2), (PEER_KEY_DIM // 2) ** -0.5)
    peer_u = nrm(31, (L, PEER_EXPERTS, D_MODEL), D_MODEL ** -0.5)
    peer_v = nrm(32, (L, PEER_EXPERTS, D_MODEL), DN_BETA)
    ln2_g = gain(33, (L, D_MODEL))
    ln2_b = nrm(34, (L, D_MODEL), 0.02)
    return {"x": x, "positions": positions, "ln_in_g": ln_in_g, "ln_in_b": ln_in_b, "w_in": w_in,
            "s5_a_re": s5_a_re, "s5_a_im": s5_a_im, "s5_log_dt": s5_log_dt, "s5_b_re": s5_b_re,
            "s5_b_im": s5_b_im, "s5_c_re": s5_c_re, "s5_c_im": s5_c_im, "s5_d": s5_d, "s5_w_glu": s5_w_glu,
            "m2_conv_w": m2_conv_w, "m2_conv_b": m2_conv_b, "m2_dt_bias": m2_dt_bias, "m2_a_log": m2_a_log,
            "m2_d": m2_d, "m2_norm_w": m2_norm_w, "mla_q_norm": mla_q_norm, "mla_w_uq": mla_w_uq,
            "mla_kv_norm": mla_kv_norm, "mla_w_ukv": mla_w_ukv, "w_br_a": w_br_a, "w_br_b": w_br_b,
            "w_br_c": w_br_c, "w_out": w_out, "ln1_g": ln1_g, "ln1_b": ln1_b, "peer_w_q": peer_w_q,
            "peer_subkeys": peer_subkeys, "peer_u": peer_u, "peer_v": peer_v, "ln2_g": ln2_g, "ln2_b": ln2_b}


def reference(x, positions, ln_in_g, ln_in_b, w_in, s5_a_re, s5_a_im, s5_log_dt, s5_b_re, s5_b_im,
              s5_c_re, s5_c_im, s5_d, s5_w_glu, m2_conv_w, m2_conv_b, m2_dt_bias, m2_a_log, m2_d,
              m2_norm_w, mla_q_norm, mla_w_uq, mla_kv_norm, mla_w_ukv, w_br_a, w_br_b, w_br_c, w_out,
              ln1_g, ln1_b, peer_w_q, peer_subkeys, peer_u, peer_v, ln2_g, ln2_b):
    cuts = np.cumsum(IN_SPLITS)[:-1].tolist()
    h = layer_norm(x, ln_in_g, ln_in_b)
    for l in range(DEPTH):
        proj = h @ w_in[l]
        u_s5, z_m2, xbc_m2, dt_m2, c_q, c_kv, k_rope, gates = jnp.split(proj, cuts, axis=-1)
        y_a = s5_mixer(u_s5, s5_a_re[l], s5_a_im[l], s5_log_dt[l], s5_b_re[l], s5_b_im[l],
                       s5_c_re[l], s5_c_im[l], s5_d[l], s5_w_glu[l])
        y_b = mamba2_mixer(z_m2, xbc_m2, dt_m2, m2_conv_w[l], m2_conv_b[l], m2_dt_bias[l],
                           m2_a_log[l], m2_d[l], m2_norm_w[l])
        y_c = mla_mixer(c_q, c_kv, k_rope, positions, mla_q_norm[l], mla_w_uq[l],
                        mla_kv_norm[l], mla_w_ukv[l])
        g_a, g_b, g_c = jnp.split(jax.nn.sigmoid(gates.astype(F32)), N_BRANCHES, axis=-1)
        merged = (g_a * (y_a @ w_br_a[l]).astype(F32) + g_b * (y_b @ w_br_b[l]).astype(F32)
                  + g_c * (y_c @ w_br_c[l]).astype(F32))
        mix = merged.astype(h.dtype) @ w_out[l]
        h = layer_norm(DN_ALPHA * h + mix, ln1_g[l], ln1_b[l])
        ffn = peer_ffn(h, peer_w_q[l], peer_subkeys[l], peer_u[l], peer_v[l])
        h = layer_norm(DN_ALPHA * h + ffn, ln2_g[l], ln2_b[l])
    return h
```

```python
import functools
import math

import numpy as np
import jax
import jax.numpy as jnp
from jax import lax
from jax.experimental import pallas as pl
from jax.experimental.pallas import tpu as pltpu

F32 = jnp.float32
BF16 = jnp.bfloat16
HIGHEST = lax.Precision.HIGHEST

D_MODEL = 1024
DEPTH = 2
S5_GROUP = 16
S5_GROUPS = 24
S5_WIDTH = S5_GROUP * S5_GROUPS
S5_STATE = 64
M2_HEADDIM = 64
M2_INNER = 512
M2_HEADS = M2_INNER // M2_HEADDIM
M2_GROUPS = 2
M2_STATE = 128
M2_CONV = 4
M2_CONV_DIM = M2_INNER + 2 * M2_GROUPS * M2_STATE
MLA_HEADS = 8
MLA_Q_RANK = 256
MLA_KV_RANK = 128
MLA_NOPE = 64
MLA_ROPE = 32
MLA_V = 64
MLA_WIDTH = MLA_HEADS * MLA_V
ROPE_THETA = 10000.0
PEER_HEADS = 8
PEER_NKEYS = 128
PEER_EXPERTS = PEER_NKEYS * PEER_NKEYS
PEER_KEY_DIM = 256
PEER_TOPK = 16
N_BRANCHES = 3
IN_SPLITS = (S5_WIDTH, M2_INNER, M2_CONV_DIM, M2_HEADS, MLA_Q_RANK, MLA_KV_RANK, MLA_ROPE, N_BRANCHES * D_MODEL)
DN_ALPHA = (2 * DEPTH) ** 0.25
LN_EPS = 1e-5
RMS_EPS = 1e-6

LANES = 128
SUBLANES = 8
BF16_ROWS = 16
VMEM_LIMIT = 56 * 1024 * 1024

S5_CHUNK = 32
SSD_CHUNK = 128
ATT_BLOCK = 512
ROW_BLOCK = 512
PEER_ROUTE_BLOCK = 256
PEER_BLOCK = 128


def _cparams(sem):
    return pltpu.CompilerParams(dimension_semantics=sem, vmem_limit_bytes=VMEM_LIMIT)


def _ln_rows(v, g, b):
    mu = jnp.mean(v, -1, keepdims=True)
    vc = v - mu
    var = jnp.mean(vc * vc, -1, keepdims=True)
    return vc * lax.rsqrt(var + LN_EPS) * g + b


def _gelu_tanh(v):
    return 0.5 * v * (1.0 + jnp.tanh(math.sqrt(2.0 / math.pi) * (v + 0.044715 * (v * v * v))))


def _sigmoid(v):
    return 1.0 / (1.0 + jnp.exp(-v))


def _silu(v):
    return v * _sigmoid(v)


def _ln_kernel(x_ref, g_ref, b_ref, o_ref):
    o_ref[...] = _ln_rows(x_ref[...], g_ref[...], b_ref[...])


def _layer_norm_in(xt, g, b):
    t, d = xt.shape
    return pl.pallas_call(
        _ln_kernel,
        grid=(t // ROW_BLOCK,),
        in_specs=[pl.BlockSpec((ROW_BLOCK, d), lambda i: (i, 0)),
                  pl.BlockSpec((1, d), lambda i: (0, 0)),
                  pl.BlockSpec((1, d), lambda i: (0, 0))],
        out_specs=pl.BlockSpec((ROW_BLOCK, d), lambda i: (i, 0)),
        out_shape=jax.ShapeDtypeStruct((t, d), F32),
        compiler_params=_cparams(("parallel",)),
        name="ln_in",
    )(xt, g.reshape(1, d), b.reshape(1, d))


def _mm_kernel(x_ref, w_ref, o_ref):
    o_ref[...] = jnp.dot(x_ref[...].astype(BF16), w_ref[...], preferred_element_type=F32).astype(o_ref.dtype)


def _matmul(x, w, out_dtype, tn, name):
    t, k = x.shape
    n = w.shape[1]
    assert n % tn == 0 and t % ROW_BLOCK == 0
    return pl.pallas_call(
        _mm_kernel,
        grid=(t // ROW_BLOCK, n // tn),
        in_specs=[pl.BlockSpec((ROW_BLOCK, k), lambda i, j: (i, 0)),
                  pl.BlockSpec((k, tn), lambda i, j: (0, j))],
        out_specs=pl.BlockSpec((ROW_BLOCK, tn), lambda i, j: (i, j)),
        out_shape=jax.ShapeDtypeStruct((t, n), out_dtype),
        compiler_params=_cparams(("parallel", "arbitrary")),
        name=name,
    )(x, w)


def _s5_operators(a_re, a_im, log_dt, b_re, b_im, c_re, c_im, chunk, n_steps):
    lr, li = a_re.astype(F32), a_im.astype(F32)
    dt = jnp.exp(log_dt.astype(F32))[:, None]

    def apow(n):
        n = jnp.asarray(n, F32)[..., None, None]
        mag = jnp.exp(lr * dt * n)
        return mag * jnp.cos(li * dt * n), mag * jnp.sin(li * dt * n)

    abar_re, abar_im = apow(1.0)
    nr, ni = abar_re - 1.0, abar_im
    den = lr * lr + li * li
    coef_re = ((nr * lr + ni * li) / den)[..., None]
    coef_im = ((ni * lr - nr * li) / den)[..., None]
    br, bi = b_re.astype(F32), b_im.astype(F32)
    bbar_re = coef_re * br - coef_im * bi
    bbar_im = coef_re * bi + coef_im * br
    cr, ci = c_re.astype(F32), c_im.astype(F32)

    lags = jnp.arange(chunk + 1)
    pr, pi = apow(lags)
    cp_re = cr[None] * pr[:, :, None, :] - ci[None] * pi[:, :, None, :]
    cp_im = cr[None] * pi[:, :, None, :] + ci[None] * pr[:, :, None, :]
    kern = (jnp.einsum('jghp,gpk->gjhk', cp_re[:chunk], bbar_re, precision=HIGHEST)
            - jnp.einsum('jghp,gpk->gjhk', cp_im[:chunk], bbar_im, precision=HIGHEST))
    tt = jnp.arange(chunk)
    lag = tt[None, :] - tt[:, None]
    kg = kern[:, jnp.clip(lag, 0, chunk - 1)]
    kg = jnp.where((lag >= 0)[None, :, :, None, None], kg, 0.0)
    g = lr.shape[0]
    w = chunk * S5_GROUP
    tmat = kg.transpose(0, 1, 4, 2, 3).reshape(g, w, w)

    rev = chunk - 1 - tt
    bm_re = pr[rev][:, :, :, None] * bbar_re[None] - pi[rev][:, :, :, None] * bbar_im[None]
    bm_im = pr[rev][:, :, :, None] * bbar_im[None] + pi[rev][:, :, :, None] * bbar_re[None]
    bmat = jnp.concatenate([bm_re, bm_im], axis=2)
    bmat = bmat.transpose(1, 0, 3, 2).reshape(g, w, 2 * S5_STATE)

    cm = jnp.concatenate([cp_re[1:], -cp_im[1:]], axis=3)
    cmat = cm.transpose(1, 3, 0, 2).reshape(g, 2 * S5_STATE, w)

    steps = chunk * (2 ** jnp.arange(n_steps))
    sr, si = apow(steps)
    par = jnp.concatenate([sr, sr], axis=-1)
    pai = jnp.concatenate([-si, si], axis=-1)
    scan = jnp.stack([par, pai], axis=2).transpose(1, 0, 2, 3)
    return tmat.astype(BF16), bmat.astype(BF16), cmat.astype(BF16), scan


def _s5_kernel(u_ref, t_ref, b_ref, c_ref, scan_ref, y_ref, *, chunks_per_seq, n_steps):
    u = u_ref[0]
    nc = u.shape[0]
    z = jnp.dot(u, b_ref[0], preferred_element_type=F32)
    row = lax.broadcasted_iota(jnp.int32, (nc, 1), 0) % chunks_per_seq
    for k in range(n_steps):
        sh = 1 << k
        zs = jnp.where(row >= sh, pltpu.roll(z, sh, axis=0), 0.0)
        zsw = pltpu.roll(zs, S5_STATE, axis=1)
        z = z + scan_ref[0, k, 0:1, :] * zs + scan_ref[0, k, 1:2, :] * zsw
    s0 = jnp.where(row >= 1, pltpu.roll(z, 1, axis=0), 0.0)
    y = jnp.dot(u, t_ref[0], preferred_element_type=F32)
    y = y + jnp.dot(s0.astype(BF16), c_ref[0], preferred_element_type=F32)
    y_ref[0] = y


def _s5_scan(u_g, tmat, bmat, cmat, scan, chunks_per_seq):
    g, nc, w = u_g.shape
    n_steps = scan.shape[1]
    return pl.pallas_call(
        functools.partial(_s5_kernel, chunks_per_seq=chunks_per_seq, n_steps=n_steps),
        grid=(g,),
        in_specs=[pl.BlockSpec((1, nc, w), lambda i: (i, 0, 0)),
                  pl.BlockSpec((1, w, w), lambda i: (i, 0, 0)),
                  pl.BlockSpec((1, w, 2 * S5_STATE), lambda i: (i, 0, 0)),
                  pl.BlockSpec((1, 2 * S5_STATE, w), lambda i: (i, 0, 0)),
                  pl.BlockSpec((1, n_steps, 2, 2 * S5_STATE), lambda i: (i, 0, 0, 0))],
        out_specs=pl.BlockSpec((1, nc, w), lambda i: (i, 0, 0)),
        out_shape=jax.ShapeDtypeStruct((g, nc, w), F32),
        compiler_params=_cparams(("parallel",)),
        name="s5_scan",
    )(u_g, tmat, bmat, cmat, scan)


def _s5_glu_kernel(y_ref, u_ref, d_ref, w_ref, o_ref):
    v = y_ref[...] + d_ref[...] * u_ref[...]
    z = _gelu_tanh(v)
    gate = _sigmoid(jnp.dot(z.astype(BF16), w_ref[...], preferred_element_type=F32))
    o_ref[...] = (z * gate).astype(o_ref.dtype)


def _s5_glu(y, u, d, w_glu):
    t, w = y.shape
    row = pl.BlockSpec((ROW_BLOCK, w), lambda i: (i, 0))
    return pl.pallas_call(
        _s5_glu_kernel,
        grid=(t // ROW_BLOCK,),
        in_specs=[row, row, pl.BlockSpec((1, w), lambda i: (0, 0)), pl.BlockSpec((w, w), lambda i: (0, 0))],
        out_specs=row,
        out_shape=jax.ShapeDtypeStruct((t, w), BF16),
        compiler_params=_cparams(("parallel",)),
        name="s5_glu",
    )(y, u, d.reshape(1, w).astype(F32), w_glu.astype(BF16))


def _s5_branch(u, seq, a_re, a_im, log_dt, b_re, b_im, c_re, c_im, d, w_glu):
    t = u.shape[0]
    chunk = S5_CHUNK
    cps = seq // chunk
    n_steps = max(1, (cps - 1).bit_length())
    tmat, bmat, cmat, scan = _s5_operators(a_re, a_im, log_dt, b_re, b_im, c_re, c_im, chunk, n_steps)
    nc = t // chunk
    u_g = (u.reshape(nc, chunk, S5_GROUPS, S5_GROUP).transpose(2, 0, 1, 3)
           .reshape(S5_GROUPS, nc, chunk * S5_GROUP).astype(BF16))
    y_g = _s5_scan(u_g, tmat, bmat, cmat, scan, cps)
    y = (y_g.reshape(S5_GROUPS, nc, chunk, S5_GROUP).transpose(1, 2, 0, 3).reshape(t, S5_WIDTH))
    return _s5_glu(y, u, d, w_glu)


def _softplus(v):
    return jnp.maximum(v, 0.0) + jnp.log1p(jnp.exp(-jnp.abs(v)))


def _ssd_kernel(zx_ref, dt_ref, dtt_ref, cw_ref, cb_ref, dtb_ref, dtbt_ref, alog_ref, alogt_ref,
                dskip_ref, nw_ref, o_ref, ext_ref, state_ref):
    L = SSD_CHUNK
    n = M2_STATE
    p = M2_HEADDIM

    @pl.when(pl.program_id(1) == 0)
    def _():
        ext_ref[0:SUBLANES, :] = jnp.zeros((SUBLANES, M2_CONV_DIM), F32)
        state_ref[...] = jnp.zeros_like(state_ref)

    z = zx_ref[:, :M2_INNER]
    xbc = zx_ref[:, M2_INNER:]
    ext_ref[SUBLANES:SUBLANES + L, :] = xbc
    conv = cb_ref[...] + cw_ref[M2_CONV - 1:M2_CONV, :] * xbc
    for k in range(M2_CONV - 1):
        lo = SUBLANES - (M2_CONV - 1) + k
        conv = conv + cw_ref[k:k + 1, :] * ext_ref[lo:lo + L, :]
    ext_ref[0:SUBLANES, :] = ext_ref[L:L + SUBLANES, :]
    xc = _silu(conv)
    xs = xc[:, :M2_INNER]
    bm = xc[:, M2_INNER:M2_INNER + M2_GROUPS * n]
    cm = xc[:, M2_INNER + M2_GROUPS * n:]

    dt = _softplus(dt_ref[...] + dtb_ref[...])
    a = -jnp.exp(alog_ref[...]) * dt
    a_t = -jnp.exp(alogt_ref[...]) * _softplus(dtt_ref[...] + dtbt_ref[...])
    ri = lax.broadcasted_iota(jnp.int32, (L, L), 0)
    ci = lax.broadcasted_iota(jnp.int32, (L, L), 1)
    causal = ci <= ri
    tril = causal.astype(F32)
    triu = (ri <= ci).astype(F32)
    acum_col = jnp.dot(tril, a, preferred_element_type=F32, precision=HIGHEST)
    acum_row = jnp.dot(a_t, triu, preferred_element_type=F32, precision=HIGHEST)

    ys = []
    rep = M2_HEADS // M2_GROUPS
    for g in range(M2_GROUPS):
        bg = bm[:, g * n:(g + 1) * n]
        cgb = cm[:, g * n:(g + 1) * n].astype(BF16)
        scores = lax.dot_general(cgb, bg.astype(BF16), (((1,), (1,)), ((), ())), preferred_element_type=F32)
        bg_t = bg.T.astype(BF16)
        for hh in range(rep):
            h = g * rep + hh
            ac = acum_col[:, h:h + 1]
            ar = acum_row[h:h + 1, :]
            decay = jnp.exp(jnp.where(causal, ac - ar, -jnp.inf))
            xh = xs[:, h * p:(h + 1) * p]
            xdt = xh * dt[:, h:h + 1]
            y = jnp.dot((scores * decay).astype(BF16), xdt.astype(BF16), preferred_element_type=F32)
            st = state_ref[h]
            y = y + jnp.dot(cgb, st.astype(BF16), preferred_element_type=F32) * jnp.exp(ac)
            y = y + dskip_ref[:, h * p:(h + 1) * p] * xh
            a_last = acum_col[L - 1:L, h:h + 1]
            xw = (xdt * jnp.exp(a_last - ac)).astype(BF16)
            state_ref[h] = jnp.exp(a_last) * st + jnp.dot(bg_t, xw, preferred_element_type=F32)
            ys.append(y)
    y = jnp.concatenate(ys, axis=1) * _silu(z)
    gw = M2_INNER // M2_GROUPS
    outs = []
    for g in range(M2_GROUPS):
        yg = y[:, g * gw:(g + 1) * gw]
        outs.append(yg * lax.rsqrt(jnp.mean(yg * yg, -1, keepdims=True) + RMS_EPS))
    o_ref[...] = (jnp.concatenate(outs, axis=1) * nw_ref[...]).astype(o_ref.dtype)


def _ssd_branch(zx, small, dt_col_block, batch, seq, conv_w, conv_b, dt_bias, a_log, d, norm_w):
    t = zx.shape[0]
    L = SSD_CHUNK
    ncs = seq // L
    hh = M2_HEADS
    dt_t = small[:, dt_col_block * LANES:dt_col_block * LANES + hh].T
    lane_pad = lambda v: jnp.zeros((1, LANES), F32).at[0, :hh].set(v.astype(F32))
    col = lambda v: v.astype(F32).reshape(hh, 1)
    dskip = jnp.repeat(d.astype(F32), M2_HEADDIM).reshape(1, M2_INNER)
    wz = M2_INNER + M2_CONV_DIM
    const = lambda shape: pl.BlockSpec(shape, lambda b, c: tuple(0 for _ in shape))
    return pl.pallas_call(
        _ssd_kernel,
        grid=(batch, ncs),
        in_specs=[pl.BlockSpec((L, wz), lambda b, c: (b * ncs + c, 0)),
                  pl.BlockSpec((L, LANES), lambda b, c: (b * ncs + c, dt_col_block)),
                  pl.BlockSpec((hh, L), lambda b, c: (0, b * ncs + c)),
                  const((M2_CONV, M2_CONV_DIM)), const((1, M2_CONV_DIM)),
                  const((1, LANES)), const((hh, 1)), const((1, LANES)), const((hh, 1)),
                  const((1, M2_INNER)), const((1, M2_INNER))],
        out_specs=pl.BlockSpec((L, M2_INNER), lambda b, c: (b * ncs + c, 0)),
        out_shape=jax.ShapeDtypeStruct((t, M2_INNER), BF16),
        scratch_shapes=[pltpu.VMEM((L + SUBLANES, M2_CONV_DIM), F32),
                        pltpu.VMEM((hh, M2_STATE, M2_HEADDIM), F32)],
        compiler_params=_cparams(("parallel", "arbitrary")),
        name="ssd",
    )(zx, small, dt_t, conv_w.astype(F32), conv_b.astype(F32).reshape(1, -1),
      lane_pad(dt_bias), col(dt_bias), lane_pad(a_log), col(a_log), dskip,
      norm_w.astype(F32).reshape(1, -1))


def _rot_cols(w):
    half = MLA_ROPE // 2
    return jnp.concatenate([-w[..., half:], w[..., :half]], axis=-1)


def _mla_prep_kernel(cq_ref, ckv_ref, krd_ref, ck_ref, sk_ref, qn_ref, kvn_ref, wq_ref, wkv_ref,
                     q_out, k_out, v_out):
    cq = cq_ref[...]
    qn = cq * lax.rsqrt(jnp.mean(cq * cq, -1, keepdims=True) + RMS_EPS) * qn_ref[...]
    q = jnp.dot(qn.astype(BF16), wq_ref[...], preferred_element_type=F32)
    ckv = ckv_ref[...]
    kvn = ckv * lax.rsqrt(jnp.mean(ckv * ckv, -1, keepdims=True) + RMS_EPS) * kvn_ref[...]
    kv = jnp.dot(kvn.astype(BF16), wkv_ref[...], preferred_element_type=F32)
    ck, sk = ck_ref[...], sk_ref[...]
    krd = krd_ref[...]
    shift = LANES - MLA_ROPE
    kr = krd * ck + pltpu.roll(krd, shift, axis=1) * sk
    scale = (MLA_NOPE + MLA_ROPE) ** -0.5
    lane = lax.broadcasted_iota(jnp.int32, (1, LANES), 1)
    cq_t = scale * (ck + (lane < MLA_NOPE).astype(F32))
    sq_t = scale * sk
    ones_col = (lane == MLA_V).astype(F32)
    for h in range(MLA_HEADS):
        qh = q[:, h * LANES:(h + 1) * LANES]
        q_out[h] = (qh * cq_t + pltpu.roll(qh, shift, axis=1) * sq_t).astype(BF16)
        k_out[h] = (kv[:, 2 * h * LANES:(2 * h + 1) * LANES] + kr).astype(BF16)
        v_out[h] = (kv[:, (2 * h + 1) * LANES:(2 * h + 2) * LANES] + ones_col).astype(BF16)


def _attn_kernel(q_ref, k_ref, v_ref, o_ref, m_ref, acc_ref):
    qi = pl.program_id(2)
    blk = ATT_BLOCK
    nt = (((1,), (1,)), ((), ()))
    outs = []
    for hh in range(2):
        q = q_ref[hh]
        m_ref[...] = jnp.full(m_ref.shape, -jnp.inf, F32)
        acc_ref[...] = jnp.zeros(acc_ref.shape, F32)

        def step(j, masked):
            k = k_ref[hh, pl.ds(pl.multiple_of(j * blk, blk), blk), :]
            v = v_ref[hh, pl.ds(pl.multiple_of(j * blk, blk), blk), :]
            s = lax.dot_general(q, k, nt, preferred_element_type=F32)
            if masked:
                ri = lax.broadcasted_iota(jnp.int32, (blk, blk), 0)
                ci = lax.broadcasted_iota(jnp.int32, (blk, blk), 1)
                s = jnp.where(ci <= ri, s, -jnp.inf)
            m_old = m_ref[...]
            m_new = jnp.maximum(m_old, jnp.max(s, axis=1, keepdims=True))
            pexp = jnp.exp(s - m_new)
            acc_ref[...] = acc_ref[...] * jnp.exp(m_old - m_new) + jnp.dot(
                pexp.astype(BF16), v, preferred_element_type=F32)
            m_ref[...] = m_new

        def body(j, carry):
            step(j, False)
            return carry

        lax.fori_loop(0, qi, body, 0)
        step(qi, True)
        acc = acc_ref[...]
        outs.append(acc[:, :MLA_V] / acc[:, MLA_V:MLA_V + 1])
    o_ref[...] = jnp.concatenate(outs, axis=1).astype(o_ref.dtype)


def _mla_branch(small, positions, batch, seq, q_norm, w_uq, kv_norm, w_ukv):
    t = small.shape[0]
    hh = MLA_HEADS
    inv_freq = 1.0 / (ROPE_THETA ** (jnp.arange(0, MLA_ROPE, 2, dtype=F32) / MLA_ROPE))
    ang = positions.astype(F32).reshape(t, 1) * inv_freq
    cos2 = jnp.concatenate([jnp.cos(ang), jnp.cos(ang)], axis=-1)
    sin2 = jnp.concatenate([jnp.sin(ang), jnp.sin(ang)], axis=-1)
    ck = jnp.zeros((t, LANES), F32).at[:, MLA_NOPE:MLA_NOPE + MLA_ROPE].set(cos2)
    sk = jnp.zeros((t, LANES), F32).at[:, MLA_NOPE:MLA_NOPE + MLA_ROPE].set(sin2)
    wq = w_uq.astype(F32).reshape(MLA_Q_RANK, hh, MLA_NOPE + MLA_ROPE)
    wq = jnp.concatenate([wq, _rot_cols(wq[..., MLA_NOPE:])], axis=-1).reshape(MLA_Q_RANK, hh * LANES)
    wkv = w_ukv.astype(F32).reshape(MLA_KV_RANK, hh, MLA_NOPE + MLA_V)
    zpad = jnp.zeros((MLA_KV_RANK, hh, LANES - MLA_NOPE), F32)
    wkv = jnp.concatenate([wkv[..., :MLA_NOPE], zpad, wkv[..., MLA_NOPE:], zpad], axis=-1)
    wkv = wkv.reshape(MLA_KV_RANK, hh * 2 * LANES)
    tm = ROW_BLOCK
    const = lambda shape: pl.BlockSpec(shape, lambda i: tuple(0 for _ in shape))
    head_out = pl.BlockSpec((hh, tm, LANES), lambda i: (0, i, 0))
    q, k, v = pl.pallas_call(
        _mla_prep_kernel,
        grid=(t // tm,),
        in_specs=[pl.BlockSpec((tm, MLA_Q_RANK), lambda i: (i, 0)),
                  pl.BlockSpec((tm, MLA_KV_RANK), lambda i: (i, MLA_Q_RANK // MLA_KV_RANK)),
                  pl.BlockSpec((tm, LANES), lambda i: (i, (MLA_Q_RANK + MLA_KV_RANK) // LANES)),
                  pl.BlockSpec((tm, LANES), lambda i: (i, 0)),
                  pl.BlockSpec((tm, LANES), lambda i: (i, 0)),
                  const((1, MLA_Q_RANK)), const((1, MLA_KV_RANK)),
                  const((MLA_Q_RANK, hh * LANES)), const((MLA_KV_RANK, hh * 2 * LANES))],
        out_specs=[head_out, head_out, head_out],
        out_shape=[jax.ShapeDtypeStruct((hh, t, LANES), BF16)] * 3,
        compiler_params=_cparams(("parallel",)),
        name="mla_prep",
    )(small, small, small, ck, sk, q_norm.astype(F32).reshape(1, -1), kv_norm.astype(F32).reshape(1, -1),
      wq.astype(BF16), wkv.astype(BF16))
    blk = ATT_BLOCK
    nq = seq // blk
    return pl.pallas_call(
        _attn_kernel,
        grid=(batch, hh // 2, nq),
        in_specs=[pl.BlockSpec((2, blk, LANES), lambda b, hp, i: (hp, b * nq + i, 0)),
                  pl.BlockSpec((2, seq, LANES), lambda b, hp, i: (hp, b, 0)),
                  pl.BlockSpec((2, seq, LANES), lambda b, hp, i: (hp, b, 0))],
        out_specs=pl.BlockSpec((blk, LANES), lambda b, hp, i: (b * nq + i, hp)),
        out_shape=jax.ShapeDtypeStruct((t, MLA_WIDTH), BF16),
        scratch_shapes=[pltpu.VMEM((blk, 1), F32), pltpu.VMEM((blk, LANES), F32)],
        compiler_params=_cparams(("parallel", "parallel", "arbitrary")),
        name="mla_attn",
    )(q, k, v)


def _merge_kernel(h_ref, ya_ref, yb_ref, yc_ref, g_ref, wa_ref, wb_ref, wc_ref, wo_ref, lg_ref, lb_ref,
                  wq_ref, h1_ref, qp_ref):
    d = D_MODEL
    gates = _sigmoid(g_ref[...].astype(F32))
    merged = (gates[:, :d] * jnp.dot(ya_ref[...], wa_ref[...], preferred_element_type=F32)
              + gates[:, d:2 * d] * jnp.dot(yb_ref[...], wb_ref[...], preferred_element_type=F32)
              + gates[:, 2 * d:] * jnp.dot(yc_ref[...], wc_ref[...], preferred_element_type=F32))
    mix = jnp.dot(merged.astype(BF16), wo_ref[...], preferred_element_type=F32)
    h1 = _ln_rows(DN_ALPHA * h_ref[...] + mix, lg_ref[...], lb_ref[...])
    h1_ref[...] = h1
    qp_ref[...] = jnp.dot(h1.astype(BF16), wq_ref[...], preferred_element_type=F32).astype(qp_ref.dtype)


def _merge(h, ya, yb, yc, gates, wa, wb, wc, wo, lg, lb, wq):
    t, d = h.shape
    tm = ROW_BLOCK
    row = lambda w: pl.BlockSpec((tm, w), lambda i: (i, 0))
    const = lambda a: pl.BlockSpec(a.shape, lambda i: (0, 0))
    ws = [wa.astype(BF16), wb.astype(BF16), wc.astype(BF16), wo.astype(BF16),
          lg.astype(F32).reshape(1, d), lb.astype(F32).reshape(1, d), wq.astype(BF16)]
    nq = wq.shape[1]
    return pl.pallas_call(
        _merge_kernel,
        grid=(t // tm,),
        in_specs=[row(d), row(ya.shape[1]), row(yb.shape[1]), row(yc.shape[1]), row(gates.shape[1])]
                 + [const(a) for a in ws],
        out_specs=[row(d), row(nq)],
        out_shape=[jax.ShapeDtypeStruct((t, d), F32), jax.ShapeDtypeStruct((t, nq), BF16)],
        compiler_params=_cparams(("parallel",)),
        name="merge",
    )(h, ya, yb, yc, gates, *ws)


_PEER_PAIRS = tuple((a, b) for a in range(PEER_TOPK) for b in range(PEER_TOPK) if (a + 1) * (b + 1) <= PEER_TOPK)


def _route_kernel(qp_ref, sk_ref, off_ref, par_ref, gate_ref, s_scr, i_scr):
    tt = PEER_ROUTE_BLOCK
    nk = PEER_NKEYS
    kd = PEER_KEY_DIM // 2
    nt = (((1,), (1,)), ((), ()))
    key_iota = lax.broadcasted_iota(jnp.int32, (nk, tt), 0)

    def head(h, carry):
        for j in range(2):
            q = qp_ref[:, pl.ds(pl.multiple_of((2 * h + j) * kd, kd), kd)]
            vals = lax.dot_general(sk_ref[j], q, nt, preferred_element_type=F32)
            for r in range(PEER_TOPK):
                m = jnp.max(vals, axis=0, keepdims=True)
                i = jnp.min(jnp.where(vals == m, key_iota, nk), axis=0, keepdims=True)
                vals = jnp.where(key_iota == i, -jnp.inf, vals)
                s_scr[j * PEER_TOPK + r, pl.ds(h, 1), :] = m
                i_scr[j * PEER_TOPK + r, pl.ds(h, 1), :] = i
        return carry

    lax.fori_loop(0, PEER_HEADS, head, 0)

    s1 = [s_scr[r] for r in range(PEER_TOPK)]
    s2 = [s_scr[PEER_TOPK + r] for r in range(PEER_TOPK)]
    i1 = [i_scr[r] for r in range(PEER_TOPK)]
    i2 = [i_scr[PEER_TOPK + r] for r in range(PEER_TOPK)]
    cand = [s1[a] + s2[b] for a, b in _PEER_PAIRS]
    cidx = [i1[a] * nk + i2[b] for a, b in _PEER_PAIRS]
    best_s, best_i = [], []
    for r in range(PEER_TOPK):
        m = functools.reduce(jnp.maximum, cand)
        found = jnp.zeros(m.shape, jnp.bool_)
        sel = jnp.zeros(m.shape, jnp.int32)
        for c in range(len(cand)):
            hit = cand[c] == m
            take = jnp.logical_and(hit, jnp.logical_not(found))
            found = jnp.logical_or(found, hit)
            sel = jnp.where(take, cidx[c], sel)
            cand[c] = jnp.where(take, -jnp.inf, cand[c])
        best_s.append(m)
        best_i.append(sel)
    e = [jnp.exp(s - best_s[0]) for s in best_s]
    tot = functools.reduce(lambda p, q: p + q, e)
    for r in range(PEER_TOPK):
        off_ref[r] = (best_i[r] >> 1) * BF16_ROWS
        par_ref[r] = (best_i[r] & 1).astype(F32)
        gate_ref[r] = e[r] / tot


def _peer_route(qp, subkeys):
    t = qp.shape[0]
    tt = PEER_ROUTE_BLOCK
    out = pl.BlockSpec((PEER_TOPK, PEER_HEADS, tt), lambda i: (0, 0, i))
    shp = (PEER_TOPK, PEER_HEADS, t)
    off, par, gate = pl.pallas_call(
        _route_kernel,
        grid=(t // tt,),
        in_specs=[pl.BlockSpec((tt, qp.shape[1]), lambda i: (i, 0)),
                  pl.BlockSpec(subkeys.shape, lambda i: (0, 0, 0))],
        out_specs=[out, out, out],
        out_shape=[jax.ShapeDtypeStruct(shp, jnp.int32), jax.ShapeDtypeStruct(shp, F32),
                   jax.ShapeDtypeStruct(shp, F32)],
        scratch_shapes=[pltpu.VMEM((2 * PEER_TOPK, PEER_HEADS, tt), F32),
                        pltpu.VMEM((2 * PEER_TOPK, PEER_HEADS, tt), jnp.int32)],
        compiler_params=_cparams(("parallel",)),
        name="peer_route",
    )(qp, subkeys.astype(BF16))
    tok_major = lambda a: a.transpose(2, 1, 0).reshape(t, PEER_HEADS * PEER_TOPK)
    return tok_major(off).reshape(-1), tok_major(par), tok_major(gate)


NSEL = PEER_HEADS * PEER_TOPK
WROWS = NSEL * BF16_ROWS


def _peer_tables(tab):
    e, d = tab.shape
    c = d // LANES
    return tab.astype(BF16).reshape(e // 2, 2, c, LANES).transpose(0, 2, 1, 3).reshape(e * c, LANES)


def _expand_consts():
    k = np.arange(WROWS) // BF16_ROWS
    r = np.arange(WROWS) % BF16_ROWS
    e = (k[None, :] == np.arange(NSEL)[:, None]).astype(np.float32)
    e2 = np.concatenate([e * (r % 2 == 0)[None, :], e * (r % 2 == 1)[None, :]], axis=0)
    return e, e2


def _gather_tiles(off_ref, tab_ref, w_scr, t):
    for k in range(NSEL):
        off = pl.multiple_of(off_ref[t * NSEL + k], BF16_ROWS)
        w_scr[k * BF16_ROWS:(k + 1) * BF16_ROWS, :] = tab_ref[pl.ds(off, BF16_ROWS), :]


def _peer_up_kernel(off_ref, x_ref, par_ref, gate_ref, tab_ref, e_ref, et_ref, act_ref, w_scr, pe_scr, zz_scr):
    tt = PEER_BLOCK
    nt = (((1,), (1,)), ((), ()))
    par = par_ref[...]
    pe_scr[...] = jnp.dot(par.astype(BF16), e_ref[...], preferred_element_type=F32)
    row_r = lax.broadcasted_iota(jnp.int32, (SUBLANES, WROWS), 1) % BF16_ROWS
    chunk = lax.broadcasted_iota(jnp.int32, (SUBLANES, WROWS), 0)
    want_even = (row_r - 2 * chunk).astype(F32)

    def tok(t, carry):
        _gather_tiles(off_ref, tab_ref, w_scr, t)
        r = lax.dot_general(x_ref[t].astype(BF16), w_scr[...], nt, preferred_element_type=F32)
        mask = want_even == pe_scr[pl.ds(t, 1), :]
        zz_scr[pl.ds(pl.multiple_of(t * SUBLANES, SUBLANES), SUBLANES), :] = jnp.where(mask, r, 0.0)
        return carry

    lax.fori_loop(0, tt, tok, 0)
    hd8 = jnp.dot(zz_scr[...].astype(BF16), et_ref[...], preferred_element_type=F32)
    hidden = jnp.sum(hd8.reshape(tt, SUBLANES, NSEL), axis=1)
    act = gate_ref[...] * _gelu_tanh(hidden)
    act_ref[:, :NSEL] = act * (1.0 - par)
    act_ref[:, NSEL:] = act * par


def _peer_down_kernel(off_ref, act_ref, h_ref, tab_ref, e2_ref, lg_ref, lb_ref, o_ref, w_scr, ae_scr, f_scr):
    tt = PEER_BLOCK
    ae_scr[...] = jnp.dot(act_ref[...].astype(BF16), e2_ref[...], preferred_element_type=F32)
    row_r = lax.broadcasted_iota(jnp.int32, (SUBLANES, WROWS), 1) % BF16_ROWS
    chunk = lax.broadcasted_iota(jnp.int32, (SUBLANES, WROWS), 0)
    m_chunk = (row_r >> 1) == chunk

    def tok(t, carry):
        _gather_tiles(off_ref, tab_ref, w_scr, t)
        lhs = jnp.where(m_chunk, ae_scr[pl.ds(t, 1), :], 0.0).astype(BF16)
        f_scr[t] = jnp.dot(lhs, w_scr[...], preferred_element_type=F32)
        return carry

    lax.fori_loop(0, tt, tok, 0)
    v = DN_ALPHA * h_ref[...] + f_scr[...]
    inv_d = 1.0 / D_MODEL
    mu = jnp.sum(jnp.sum(v, axis=2, keepdims=True), axis=1, keepdims=True) * inv_d
    vc = v - mu
    var = jnp.sum(jnp.sum(vc * vc, axis=2, keepdims=True), axis=1, keepdims=True) * inv_d
    o_ref[...] = vc * lax.rsqrt(var + LN_EPS) * lg_ref[...] + lb_ref[...]


def _peer(h1, qp, subkeys, u_tab, v_tab, lg, lb):
    t, d = h1.shape
    tt = PEER_BLOCK
    c = d // LANES
    off, par, gate = _peer_route(qp, subkeys)
    e, e2 = _expand_consts()
    e_bf, et_bf, e2_bf = jnp.asarray(e, BF16), jnp.asarray(e.T, BF16), jnp.asarray(e2, BF16)
    utab, vtab = _peer_tables(u_tab), _peer_tables(v_tab)
    h3 = h1.reshape(t, c, LANES)
    smem = pl.BlockSpec((tt * NSEL,), lambda i: (i,), memory_space=pltpu.SMEM)
    resident = lambda a: pl.BlockSpec(a.shape, lambda i: tuple(0 for _ in a.shape), pipeline_mode=pl.Buffered(1))
    rows = lambda w: pl.BlockSpec((tt, w), lambda i: (i, 0))
    tiles = pl.BlockSpec((tt, c, LANES), lambda i: (i, 0, 0))
    act = pl.pallas_call(
        _peer_up_kernel,
        grid=(t // tt,),
        in_specs=[smem, tiles, rows(NSEL), rows(NSEL), resident(utab), resident(e_bf), resident(et_bf)],
        out_specs=rows(2 * NSEL),
        out_shape=jax.ShapeDtypeStruct((t, 2 * NSEL), F32),
        scratch_shapes=[pltpu.VMEM((WROWS, LANES), BF16), pltpu.VMEM((tt, WROWS), F32),
                        pltpu.VMEM((tt * SUBLANES, WROWS), F32)],
        compiler_params=_cparams(("parallel",)),
        name="peer_up",
    )(off, h3, par, gate, utab, e_bf, et_bf)
    lg3 = lg.astype(F32).reshape(1, c, LANES)
    lb3 = lb.astype(F32).reshape(1, c, LANES)
    out = pl.pallas_call(
        _peer_down_kernel,
        grid=(t // tt,),
        in_specs=[smem, rows(2 * NSEL), tiles, resident(vtab), resident(e2_bf), resident(lg3), resident(lb3)],
        out_specs=tiles,
        out_shape=jax.ShapeDtypeStruct((t, c, LANES), F32),
        scratch_shapes=[pltpu.VMEM((WROWS, LANES), BF16), pltpu.VMEM((tt, WROWS), F32),
                        pltpu.VMEM((tt, c, LANES), F32)],
        compiler_params=_cparams(("parallel",)),
        name="peer_down",
    )(off, act, h3, vtab, e2_bf, lg3, lb3)
    return out.reshape(t, d)


def kernel(x, positions, ln_in_g, ln_in_b, w_in, s5_a_re, s5_a_im, s5_log_dt, s5_b_re, s5_b_im, s5_c_re, s5_c_im, s5_d, s5_w_glu, m2_conv_w, m2_conv_b, m2_dt_bias, m2_a_log, m2_d, m2_norm_w, mla_q_norm, mla_w_uq, mla_kv_norm, mla_w_ukv, w_br_a, w_br_b, w_br_c, w_out, ln1_g, ln1_b, peer_w_q, peer_subkeys, peer_u, peer_v, ln2_g, ln2_b):
    b, s, d = x.shape
    t = b * s
    assert d == D_MODEL and s % ATT_BLOCK == 0 and s % SSD_CHUNK == 0 and t % ROW_BLOCK == 0
    c_s5, c_z, c_xbc, c_dt, c_q, c_kv, c_kr, _ = np.cumsum(IN_SPLITS).tolist()
    h = _layer_norm_in(x.reshape(t, d).astype(F32), ln_in_g.astype(F32), ln_in_b.astype(F32))
    for l in range(DEPTH):
        w = w_in[l].astype(F32)
        w_kr = w[:, c_kv:c_kr]
        w_small = jnp.concatenate(
            [w[:, c_dt:c_kv], w[:, c_xbc:c_dt], jnp.zeros((d, LANES // 2 - M2_HEADS), F32), w_kr, _rot_cols(w_kr)],
            axis=1)
        u = _matmul(h, w[:, :c_s5].astype(BF16), F32, S5_WIDTH, "proj_s5")
        zx = _matmul(h, w[:, c_s5:c_xbc].astype(BF16), F32, (c_xbc - c_s5) // 2, "proj_ssd")
        small = _matmul(h, w_small.astype(BF16), F32, w_small.shape[1], "proj_small")
        gates = _matmul(h, w[:, c_kr:].astype(BF16), BF16, D_MODEL, "proj_gates")
        ya = _s5_branch(u, s, s5_a_re[l], s5_a_im[l], s5_log_dt[l], s5_b_re[l], s5_b_im[l],
                        s5_c_re[l], s5_c_im[l], s5_d[l], s5_w_glu[l])
        yb = _ssd_branch(zx, small, (MLA_Q_RANK + MLA_KV_RANK) // LANES, b, s, m2_conv_w[l], m2_conv_b[l],
                         m2_dt_bias[l], m2_a_log[l], m2_d[l], m2_norm_w[l])
        yc = _mla_branch(small, positions, b, s, mla_q_norm[l], mla_w_uq[l], mla_kv_norm[l], mla_w_ukv[l])
        h1, qp = _merge(h, ya, yb, yc, gates, w_br_a[l], w_br_b[l], w_br_c[l], w_out[l], ln1_g[l], ln1_b[l],
                        peer_w_q[l])
        h = _peer(h1, qp, peer_subkeys[l], peer_u[l], peer_v[l], ln2_g[l], ln2_b[l])
    return h.reshape(b, s, d).astype(x.dtype)
```

```python
import functools
import math

import numpy as np
import jax
import jax.numpy as jnp
from jax import lax
from jax.experimental import pallas as pl
from jax.experimental.pallas import tpu as pltpu

F32 = jnp.float32
BF16 = jnp.bfloat16
HIGHEST = lax.Precision.HIGHEST

D_MODEL = 1024
DEPTH = 2
S5_GROUP = 16
S5_GROUPS = 24
S5_WIDTH = S5_GROUP * S5_GROUPS
S5_STATE = 64
M2_HEADDIM = 64
M2_INNER = 512
M2_HEADS = M2_INNER // M2_HEADDIM
M2_GROUPS = 2
M2_STATE = 128
M2_CONV = 4
M2_CONV_DIM = M2_INNER + 2 * M2_GROUPS * M2_STATE
MLA_HEADS = 8
MLA_Q_RANK = 256
MLA_KV_RANK = 128
MLA_NOPE = 64
MLA_ROPE = 32
MLA_V = 64
MLA_WIDTH = MLA_HEADS * MLA_V
ROPE_THETA = 10000.0
PEER_HEADS = 8
PEER_NKEYS = 128
PEER_EXPERTS = PEER_NKEYS * PEER_NKEYS
PEER_KEY_DIM = 256
PEER_TOPK = 16
N_BRANCHES = 3
IN_SPLITS = (S5_WIDTH, M2_INNER, M2_CONV_DIM, M2_HEADS, MLA_Q_RANK, MLA_KV_RANK, MLA_ROPE, N_BRANCHES * D_MODEL)
DN_ALPHA = (2 * DEPTH) ** 0.25
LN_EPS = 1e-5
RMS_EPS = 1e-6

LANES = 128
SUBLANES = 8
BF16_ROWS = 16
VMEM_LIMIT = 56 * 1024 * 1024

S5_CHUNK = 32
SSD_CHUNK = 128
ATT_BLOCK = 512
ROW_BLOCK = 512
PEER_ROUTE_BLOCK = 256
PEER_BLOCK = 128


def _cparams(sem):
    return pltpu.CompilerParams(dimension_semantics=sem, vmem_limit_bytes=VMEM_LIMIT)


def _ln_rows(v, g, b):
    mu = jnp.mean(v, -1, keepdims=True)
    vc = v - mu
    var = jnp.mean(vc * vc, -1, keepdims=True)
    return vc * lax.rsqrt(var + LN_EPS) * g + b


def _gelu_tanh(v):
    return 0.5 * v * (1.0 + jnp.tanh(math.sqrt(2.0 / math.pi) * (v + 0.044715 * (v * v * v))))


def _sigmoid(v):
    return 1.0 / (1.0 + jnp.exp(-v))


def _silu(v):
    return v * _sigmoid(v)


def _ln_kernel(x_ref, g_ref, b_ref, o_ref):
    o_ref[...] = _ln_rows(x_ref[...], g_ref[...], b_ref[...])


def _layer_norm_in(xt, g, b):
    t, d = xt.shape
    return pl.pallas_call(
        _ln_kernel,
        grid=(t // ROW_BLOCK,),
        in_specs=[pl.BlockSpec((ROW_BLOCK, d), lambda i: (i, 0)),
                  pl.BlockSpec((1, d), lambda i: (0, 0)),
                  pl.BlockSpec((1, d), lambda i: (0, 0))],
        out_specs=pl.BlockSpec((ROW_BLOCK, d), lambda i: (i, 0)),
        out_shape=jax.ShapeDtypeStruct((t, d), F32),
        compiler_params=_cparams(("parallel",)),
        name="ln_in",
    )(xt, g.reshape(1, d), b.reshape(1, d))


def _mm_kernel(x_ref, w_ref, o_ref):
    o_ref[...] = jnp.dot(x_ref[...].astype(BF16), w_ref[...], preferred_element_type=F32).astype(o_ref.dtype)


def _matmul(x, w, out_dtype, tn, name):
    t, k = x.shape
    n = w.shape[1]
    assert n % tn == 0 and t % ROW_BLOCK == 0
    return pl.pallas_call(
        _mm_kernel,
        grid=(t // ROW_BLOCK, n // tn),
        in_specs=[pl.BlockSpec((ROW_BLOCK, k), lambda i, j: (i, 0)),
                  pl.BlockSpec((k, tn), lambda i, j: (0, j))],
        out_specs=pl.BlockSpec((ROW_BLOCK, tn), lambda i, j: (i, j)),
        out_shape=jax.ShapeDtypeStruct((t, n), out_dtype),
        compiler_params=_cparams(("parallel", "arbitrary")),
        name=name,
    )(x, w)


def _s5_operators(a_re, a_im, log_dt, b_re, b_im, c_re, c_im, chunk, n_steps):
    lr, li = a_re.astype(F32), a_im.astype(F32)
    dt = jnp.exp(log_dt.astype(F32))[:, None]

    def apow(n):
        n = jnp.asarray(n, F32)[..., None, None]
        mag = jnp.exp(lr * dt * n)
        return mag * jnp.cos(li * dt * n), mag * jnp.sin(li * dt * n)

    abar_re, abar_im = apow(1.0)
    nr, ni = abar_re - 1.0, abar_im
    den = lr * lr + li * li
    coef_re = ((nr * lr + ni * li) / den)[..., None]
    coef_im = ((ni * lr - nr * li) / den)[..., None]
    br, bi = b_re.astype(F32), b_im.astype(F32)
    bbar_re = coef_re * br - coef_im * bi
    bbar_im = coef_re * bi + coef_im * br
    cr, ci = c_re.astype(F32), c_im.astype(F32)

    lags = jnp.arange(chunk + 1)
    pr, pi = apow(lags)
    cp_re = cr[None] * pr[:, :, None, :] - ci[None] * pi[:, :, None, :]
    cp_im = cr[None] * pi[:, :, None, :] + ci[None] * pr[:, :, None, :]
    g = lr.shape[0]
    w = chunk * S5_GROUP
    kern = (jnp.einsum('jghp,gpk->gkjh', cp_re[:chunk], bbar_re, precision=HIGHEST)
            - jnp.einsum('jghp,gpk->gkjh', cp_im[:chunk], bbar_im, precision=HIGHEST))
    kern = kern.reshape(g, S5_GROUP, w)
    tt = jnp.arange(chunk)
    tmat = jnp.stack([jnp.pad(kern[:, :, :w - s * S5_GROUP], ((0, 0), (0, 0), (s * S5_GROUP, 0)))
                      for s in range(chunk)], axis=1).reshape(g, w, w)

    rev = chunk - 1 - tt
    bm_re = pr[rev][:, :, :, None] * bbar_re[None] - pi[rev][:, :, :, None] * bbar_im[None]
    bm_im = pr[rev][:, :, :, None] * bbar_im[None] + pi[rev][:, :, :, None] * bbar_re[None]
    bmat = jnp.concatenate([bm_re, bm_im], axis=2)
    bmat = bmat.transpose(1, 0, 3, 2).reshape(g, w, 2 * S5_STATE)

    cm = jnp.concatenate([cp_re[1:], -cp_im[1:]], axis=3)
    cmat = cm.transpose(1, 3, 0, 2).reshape(g, 2 * S5_STATE, w)

    steps = chunk * (2 ** jnp.arange(n_steps))
    sr, si = apow(steps)
    par = jnp.concatenate([sr, sr], axis=-1)
    pai = jnp.concatenate([-si, si], axis=-1)
    scan = jnp.stack([par, pai], axis=2).transpose(1, 0, 2, 3)
    return tmat.astype(BF16), bmat.astype(BF16), cmat.astype(BF16), scan


def _s5_kernel(u_ref, t_ref, b_ref, c_ref, scan_ref, y_ref, *, chunks_per_seq, n_steps):
    u = u_ref[0]
    nc = u.shape[0]
    z = jnp.dot(u, b_ref[0], preferred_element_type=F32)
    row = lax.broadcasted_iota(jnp.int32, (nc, 1), 0) % chunks_per_seq
    for k in range(n_steps):
        sh = 1 << k
        zs = jnp.where(row >= sh, pltpu.roll(z, sh, axis=0), 0.0)
        zsw = pltpu.roll(zs, S5_STATE, axis=1)
        z = z + scan_ref[0, k, 0:1, :] * zs + scan_ref[0, k, 1:2, :] * zsw
    s0 = jnp.where(row >= 1, pltpu.roll(z, 1, axis=0), 0.0)
    y = jnp.dot(u, t_ref[0], preferred_element_type=F32)
    y = y + jnp.dot(s0.astype(BF16), c_ref[0], preferred_element_type=F32)
    y_ref[0] = y


def _s5_scan(u_g, tmat, bmat, cmat, scan, chunks_per_seq):
    g, nc, w = u_g.shape
    n_steps = scan.shape[1]
    return pl.pallas_call(
        functools.partial(_s5_kernel, chunks_per_seq=chunks_per_seq, n_steps=n_steps),
        grid=(g,),
        in_specs=[pl.BlockSpec((1, nc, w), lambda i: (i, 0, 0)),
                  pl.BlockSpec((1, w, w), lambda i: (i, 0, 0)),
                  pl.BlockSpec((1, w, 2 * S5_STATE), lambda i: (i, 0, 0)),
                  pl.BlockSpec((1, 2 * S5_STATE, w), lambda i: (i, 0, 0)),
                  pl.BlockSpec((1, n_steps, 2, 2 * S5_STATE), lambda i: (i, 0, 0, 0))],
        out_specs=pl.BlockSpec((1, nc, w), lambda i: (i, 0, 0)),
        out_shape=jax.ShapeDtypeStruct((g, nc, w), F32),
        compiler_params=_cparams(("parallel",)),
        name="s5_scan",
    )(u_g, tmat, bmat, cmat, scan)


def _s5_glu_kernel(y_ref, u_ref, d_ref, w_ref, o_ref):
    v = y_ref[...] + d_ref[...] * u_ref[...]
    z = _gelu_tanh(v)
    gate = _sigmoid(jnp.dot(z.astype(BF16), w_ref[...], preferred_element_type=F32))
    o_ref[...] = (z * gate).astype(o_ref.dtype)


def _s5_glu(y, u, d, w_glu):
    t, w = y.shape
    row = pl.BlockSpec((ROW_BLOCK, w), lambda i: (i, 0))
    return pl.pallas_call(
        _s5_glu_kernel,
        grid=(t // ROW_BLOCK,),
        in_specs=[row, row, pl.BlockSpec((1, w), lambda i: (0, 0)), pl.BlockSpec((w, w), lambda i: (0, 0))],
        out_specs=row,
        out_shape=jax.ShapeDtypeStruct((t, w), BF16),
        compiler_params=_cparams(("parallel",)),
        name="s5_glu",
    )(y, u, d.reshape(1, w).astype(F32), w_glu.astype(BF16))


def _s5_branch(u, seq, a_re, a_im, log_dt, b_re, b_im, c_re, c_im, d, w_glu):
    t = u.shape[0]
    chunk = S5_CHUNK
    cps = seq // chunk
    n_steps = max(1, (cps - 1).bit_length())
    tmat, bmat, cmat, scan = _s5_operators(a_re, a_im, log_dt, b_re, b_im, c_re, c_im, chunk, n_steps)
    nc = t // chunk
    u_g = (u.reshape(nc, chunk, S5_GROUPS, S5_GROUP).transpose(2, 0, 1, 3)
           .reshape(S5_GROUPS, nc, chunk * S5_GROUP).astype(BF16))
    y_g = _s5_scan(u_g, tmat, bmat, cmat, scan, cps)
    y = (y_g.reshape(S5_GROUPS, nc, chunk, S5_GROUP).transpose(1, 2, 0, 3).reshape(t, S5_WIDTH))
    return _s5_glu(y, u, d, w_glu)


def _softplus(v):
    return jnp.maximum(v, 0.0) + jnp.log1p(jnp.exp(-jnp.abs(v)))


def _ssd_kernel(zx_ref, dt_ref, dtt_ref, cw_ref, cb_ref, dtb_ref, dtbt_ref, alog_ref, alogt_ref,
                dskip_ref, nw_ref, o_ref, ext_ref, state_ref):
    L = SSD_CHUNK
    n = M2_STATE
    p = M2_HEADDIM

    @pl.when(pl.program_id(1) == 0)
    def _():
        ext_ref[0:SUBLANES, :] = jnp.zeros((SUBLANES, M2_CONV_DIM), F32)
        state_ref[...] = jnp.zeros_like(state_ref)

    z = zx_ref[:, :M2_INNER]
    xbc = zx_ref[:, M2_INNER:]
    ext_ref[SUBLANES:SUBLANES + L, :] = xbc
    conv = cb_ref[...] + cw_ref[M2_CONV - 1:M2_CONV, :] * xbc
    for k in range(M2_CONV - 1):
        lo = SUBLANES - (M2_CONV - 1) + k
        conv = conv + cw_ref[k:k + 1, :] * ext_ref[lo:lo + L, :]
    ext_ref[0:SUBLANES, :] = ext_ref[L:L + SUBLANES, :]
    xc = _silu(conv)
    xs = xc[:, :M2_INNER]
    bm = xc[:, M2_INNER:M2_INNER + M2_GROUPS * n]
    cm = xc[:, M2_INNER + M2_GROUPS * n:]

    dt = _softplus(dt_ref[...] + dtb_ref[...])
    a = -jnp.exp(alog_ref[...]) * dt
    a_t = -jnp.exp(alogt_ref[...]) * _softplus(dtt_ref[...] + dtbt_ref[...])
    ri = lax.broadcasted_iota(jnp.int32, (L, L), 0)
    ci = lax.broadcasted_iota(jnp.int32, (L, L), 1)
    causal = ci <= ri
    tril = causal.astype(F32)
    triu = (ri <= ci).astype(F32)
    acum_col = jnp.dot(tril, a, preferred_element_type=F32, precision=HIGHEST)
    acum_row = jnp.dot(a_t, triu, preferred_element_type=F32, precision=HIGHEST)

    ys = []
    rep = M2_HEADS // M2_GROUPS
    for g in range(M2_GROUPS):
        bg = bm[:, g * n:(g + 1) * n]
        cgb = cm[:, g * n:(g + 1) * n].astype(BF16)
        scores = lax.dot_general(cgb, bg.astype(BF16), (((1,), (1,)), ((), ())), preferred_element_type=F32)
        bg_t = bg.T.astype(BF16)
        for hh in range(rep):
            h = g * rep + hh
            ac = acum_col[:, h:h + 1]
            ar = acum_row[h:h + 1, :]
            decay = jnp.exp(jnp.where(causal, ac - ar, -jnp.inf))
            xh = xs[:, h * p:(h + 1) * p]
            xdt = xh * dt[:, h:h + 1]
            y = jnp.dot((scores * decay).astype(BF16), xdt.astype(BF16), preferred_element_type=F32)
            st = state_ref[h]
            y = y + jnp.dot(cgb, st.astype(BF16), preferred_element_type=F32) * jnp.exp(ac)
            y = y + dskip_ref[:, h * p:(h + 1) * p] * xh
            a_last = acum_col[L - 1:L, h:h + 1]
            xw = (xdt * jnp.exp(a_last - ac)).astype(BF16)
            state_ref[h] = jnp.exp(a_last) * st + jnp.dot(bg_t, xw, preferred_element_type=F32)
            ys.append(y)
    y = jnp.concatenate(ys, axis=1) * _silu(z)
    gw = M2_INNER // M2_GROUPS
    outs = []
    for g in range(M2_GROUPS):
        yg = y[:, g * gw:(g + 1) * gw]
        outs.append(yg * lax.rsqrt(jnp.mean(yg * yg, -1, keepdims=True) + RMS_EPS))
    o_ref[...] = (jnp.concatenate(outs, axis=1) * nw_ref[...]).astype(o_ref.dtype)


def _ssd_branch(zx, small, dt_col_block, batch, seq, conv_w, conv_b, dt_bias, a_log, d, norm_w):
    t = zx.shape[0]
    L = SSD_CHUNK
    ncs = seq // L
    hh = M2_HEADS
    dt_t = small[:, dt_col_block * LANES:dt_col_block * LANES + hh].T
    lane_pad = lambda v: jnp.zeros((1, LANES), F32).at[0, :hh].set(v.astype(F32))
    col = lambda v: v.astype(F32).reshape(hh, 1)
    dskip = jnp.repeat(d.astype(F32), M2_HEADDIM).reshape(1, M2_INNER)
    wz = M2_INNER + M2_CONV_DIM
    const = lambda shape: pl.BlockSpec(shape, lambda b, c: tuple(0 for _ in shape))
    return pl.pallas_call(
        _ssd_kernel,
        grid=(batch, ncs),
        in_specs=[pl.BlockSpec((L, wz), lambda b, c: (b * ncs + c, 0)),
                  pl.BlockSpec((L, LANES), lambda b, c: (b * ncs + c, dt_col_block)),
                  pl.BlockSpec((hh, L), lambda b, c: (0, b * ncs + c)),
                  const((M2_CONV, M2_CONV_DIM)), const((1, M2_CONV_DIM)),
                  const((1, LANES)), const((hh, 1)), const((1, LANES)), const((hh, 1)),
                  const((1, M2_INNER)), const((1, M2_INNER))],
        out_specs=pl.BlockSpec((L, M2_INNER), lambda b, c: (b * ncs + c, 0)),
        out_shape=jax.ShapeDtypeStruct((t, M2_INNER), BF16),
        scratch_shapes=[pltpu.VMEM((L + SUBLANES, M2_CONV_DIM), F32),
                        pltpu.VMEM((hh, M2_STATE, M2_HEADDIM), F32)],
        compiler_params=_cparams(("parallel", "arbitrary")),
        name="ssd",
    )(zx, small, dt_t, conv_w.astype(F32), conv_b.astype(F32).reshape(1, -1),
      lane_pad(dt_bias), col(dt_bias), lane_pad(a_log), col(a_log), dskip,
      norm_w.astype(F32).reshape(1, -1))


def _rot_cols(w):
    half = MLA_ROPE // 2
    return jnp.concatenate([-w[..., half:], w[..., :half]], axis=-1)


def _mla_prep_kernel(cq_ref, ckv_ref, krd_ref, ck_ref, sk_ref, qn_ref, kvn_ref, wq_ref, wkv_ref,
                     q_out, k_out, v_out):
    cq = cq_ref[...]
    qn = cq * lax.rsqrt(jnp.mean(cq * cq, -1, keepdims=True) + RMS_EPS) * qn_ref[...]
    q = jnp.dot(qn.astype(BF16), wq_ref[...], preferred_element_type=F32)
    ckv = ckv_ref[...]
    kvn = ckv * lax.rsqrt(jnp.mean(ckv * ckv, -1, keepdims=True) + RMS_EPS) * kvn_ref[...]
    kv = jnp.dot(kvn.astype(BF16), wkv_ref[...], preferred_element_type=F32)
    ck, sk = ck_ref[...], sk_ref[...]
    krd = krd_ref[...]
    shift = LANES - MLA_ROPE
    kr = krd * ck + pltpu.roll(krd, shift, axis=1) * sk
    scale = (MLA_NOPE + MLA_ROPE) ** -0.5
    lane = lax.broadcasted_iota(jnp.int32, (1, LANES), 1)
    cq_t = scale * (ck + (lane < MLA_NOPE).astype(F32))
    sq_t = scale * sk
    ones_col = (lane == MLA_V).astype(F32)
    for h in range(MLA_HEADS):
        qh = q[:, h * LANES:(h + 1) * LANES]
        q_out[h] = (qh * cq_t + pltpu.roll(qh, shift, axis=1) * sq_t).astype(BF16)
        k_out[h] = (kv[:, 2 * h * LANES:(2 * h + 1) * LANES] + kr).astype(BF16)
        v_out[h] = (kv[:, (2 * h + 1) * LANES:(2 * h + 2) * LANES] + ones_col).astype(BF16)


def _attn_kernel(q_ref, k_ref, v_ref, o_ref, m_ref, acc_ref):
    qi = pl.program_id(2)
    blk = ATT_BLOCK
    nt = (((1,), (1,)), ((), ()))
    outs = []
    for hh in range(2):
        q = q_ref[hh]
        m_ref[...] = jnp.full(m_ref.shape, -jnp.inf, F32)
        acc_ref[...] = jnp.zeros(acc_ref.shape, F32)

        def step(j, masked):
            k = k_ref[hh, pl.ds(pl.multiple_of(j * blk, blk), blk), :]
            v = v_ref[hh, pl.ds(pl.multiple_of(j * blk, blk), blk), :]
            s = lax.dot_general(q, k, nt, preferred_element_type=F32)
            if masked:
                ri = lax.broadcasted_iota(jnp.int32, (blk, blk), 0)
                ci = lax.broadcasted_iota(jnp.int32, (blk, blk), 1)
                s = jnp.where(ci <= ri, s, -jnp.inf)
            m_old = m_ref[...]
            m_new = jnp.maximum(m_old, jnp.max(s, axis=1, keepdims=True))
            pexp = jnp.exp(s - m_new)
            acc_ref[...] = acc_ref[...] * jnp.exp(m_old - m_new) + jnp.dot(
                pexp.astype(BF16), v, preferred_element_type=F32)
            m_ref[...] = m_new

        def body(j, carry):
            step(j, False)
            return carry

        lax.fori_loop(0, qi, body, 0)
        step(qi, True)
        acc = acc_ref[...]
        outs.append(acc[:, :MLA_V] / acc[:, MLA_V:MLA_V + 1])
    o_ref[...] = jnp.concatenate(outs, axis=1).astype(o_ref.dtype)


def _mla_branch(small, positions, batch, seq, q_norm, w_uq, kv_norm, w_ukv):
    t = small.shape[0]
    hh = MLA_HEADS
    inv_freq = 1.0 / (ROPE_THETA ** (jnp.arange(0, MLA_ROPE, 2, dtype=F32) / MLA_ROPE))
    ang = positions.astype(F32).reshape(t, 1) * inv_freq
    cos2 = jnp.concatenate([jnp.cos(ang), jnp.cos(ang)], axis=-1)
    sin2 = jnp.concatenate([jnp.sin(ang), jnp.sin(ang)], axis=-1)
    ck = jnp.zeros((t, LANES), F32).at[:, MLA_NOPE:MLA_NOPE + MLA_ROPE].set(cos2)
    sk = jnp.zeros((t, LANES), F32).at[:, MLA_NOPE:MLA_NOPE + MLA_ROPE].set(sin2)
    wq = w_uq.astype(F32).reshape(MLA_Q_RANK, hh, MLA_NOPE + MLA_ROPE)
    wq = jnp.concatenate([wq, _rot_cols(wq[..., MLA_NOPE:])], axis=-1).reshape(MLA_Q_RANK, hh * LANES)
    wkv = w_ukv.astype(F32).reshape(MLA_KV_RANK, hh, MLA_NOPE + MLA_V)
    zpad = jnp.zeros((MLA_KV_RANK, hh, LANES - MLA_NOPE), F32)
    wkv = jnp.concatenate([wkv[..., :MLA_NOPE], zpad, wkv[..., MLA_NOPE:], zpad], axis=-1)
    wkv = wkv.reshape(MLA_KV_RANK, hh * 2 * LANES)
    tm = ROW_BLOCK
    const = lambda shape: pl.BlockSpec(shape, lambda i: tuple(0 for _ in shape))
    head_out = pl.BlockSpec((hh, tm, LANES), lambda i: (0, i, 0))
    q, k, v = pl.pallas_call(
        _mla_prep_kernel,
        grid=(t // tm,),
        in_specs=[pl.BlockSpec((tm, MLA_Q_RANK), lambda i: (i, 0)),
                  pl.BlockSpec((tm, MLA_KV_RANK), lambda i: (i, MLA_Q_RANK // MLA_KV_RANK)),
                  pl.BlockSpec((tm, LANES), lambda i: (i, (MLA_Q_RANK + MLA_KV_RANK) // LANES)),
                  pl.BlockSpec((tm, LANES), lambda i: (i, 0)),
                  pl.BlockSpec((tm, LANES), lambda i: (i, 0)),
                  const((1, MLA_Q_RANK)), const((1, MLA_KV_RANK)),
                  const((MLA_Q_RANK, hh * LANES)), const((MLA_KV_RANK, hh * 2 * LANES))],
        out_specs=[head_out, head_out, head_out],
        out_shape=[jax.ShapeDtypeStruct((hh, t, LANES), BF16)] * 3,
        compiler_params=_cparams(("parallel",)),
        name="mla_prep",
    )(small, small, small, ck, sk, q_norm.astype(F32).reshape(1, -1), kv_norm.astype(F32).reshape(1, -1),
      wq.astype(BF16), wkv.astype(BF16))
    blk = ATT_BLOCK
    nq = seq // blk
    return pl.pallas_call(
        _attn_kernel,
        grid=(batch, hh // 2, nq),
        in_specs=[pl.BlockSpec((2, blk, LANES), lambda b, hp, i: (hp, b * nq + i, 0)),
                  pl.BlockSpec((2, seq, LANES), lambda b, hp, i: (hp, b, 0)),
                  pl.BlockSpec((2, seq, LANES), lambda b, hp, i: (hp, b, 0))],
        out_specs=pl.BlockSpec((blk, LANES), lambda b, hp, i: (b * nq + i, hp)),
        out_shape=jax.ShapeDtypeStruct((t, MLA_WIDTH), BF16),
        scratch_shapes=[pltpu.VMEM((blk, 1), F32), pltpu.VMEM((blk, LANES), F32)],
        compiler_params=_cparams(("parallel", "parallel", "arbitrary")),
        name="mla_attn",
    )(q, k, v)


def _merge_kernel(h_ref, ya_ref, yb_ref, yc_ref, g_ref, wa_ref, wb_ref, wc_ref, wo_ref, lg_ref, lb_ref,
                  wq_ref, h1_ref, qp_ref):
    d = D_MODEL
    gates = _sigmoid(g_ref[...].astype(F32))
    merged = (gates[:, :d] * jnp.dot(ya_ref[...], wa_ref[...], preferred_element_type=F32)
              + gates[:, d:2 * d] * jnp.dot(yb_ref[...], wb_ref[...], preferred_element_type=F32)
              + gates[:, 2 * d:] * jnp.dot(yc_ref[...], wc_ref[...], preferred_element_type=F32))
    mix = jnp.dot(merged.astype(BF16), wo_ref[...], preferred_element_type=F32)
    h1 = _ln_rows(DN_ALPHA * h_ref[...] + mix, lg_ref[...], lb_ref[...])
    h1_ref[...] = h1
    qp_ref[...] = jnp.dot(h1.astype(BF16), wq_ref[...], preferred_element_type=F32).astype(qp_ref.dtype)


def _merge(h, ya, yb, yc, gates, wa, wb, wc, wo, lg, lb, wq):
    t, d = h.shape
    tm = ROW_BLOCK
    row = lambda w: pl.BlockSpec((tm, w), lambda i: (i, 0))
    const = lambda a: pl.BlockSpec(a.shape, lambda i: (0, 0))
    ws = [wa.astype(BF16), wb.astype(BF16), wc.astype(BF16), wo.astype(BF16),
          lg.astype(F32).reshape(1, d), lb.astype(F32).reshape(1, d), wq.astype(BF16)]
    nq = wq.shape[1]
    return pl.pallas_call(
        _merge_kernel,
        grid=(t // tm,),
        in_specs=[row(d), row(ya.shape[1]), row(yb.shape[1]), row(yc.shape[1]), row(gates.shape[1])]
                 + [const(a) for a in ws],
        out_specs=[row(d), row(nq)],
        out_shape=[jax.ShapeDtypeStruct((t, d), F32), jax.ShapeDtypeStruct((t, nq), BF16)],
        compiler_params=_cparams(("parallel",)),
        name="merge",
    )(h, ya, yb, yc, gates, *ws)


_PEER_PAIRS = tuple((a, b) for a in range(PEER_TOPK) for b in range(PEER_TOPK) if (a + 1) * (b + 1) <= PEER_TOPK)


def _route_kernel(qp_ref, sk_ref, off_ref, par_ref, gate_ref, s_scr, i_scr):
    tt = PEER_ROUTE_BLOCK
    nk = PEER_NKEYS
    kd = PEER_KEY_DIM // 2
    nt = (((1,), (1,)), ((), ()))
    key_iota = lax.broadcasted_iota(jnp.int32, (nk, tt), 0)

    def head(h, carry):
        for j in range(2):
            q = qp_ref[:, pl.ds(pl.multiple_of((2 * h + j) * kd, kd), kd)]
            vals = lax.dot_general(sk_ref[j], q, nt, preferred_element_type=F32)
            for r in range(PEER_TOPK):
                m = jnp.max(vals, axis=0, keepdims=True)
                i = jnp.min(jnp.where(vals == m, key_iota, nk), axis=0, keepdims=True)
                vals = jnp.where(key_iota == i, -jnp.inf, vals)
                s_scr[j * PEER_TOPK + r, pl.ds(h, 1), :] = m
                i_scr[j * PEER_TOPK + r, pl.ds(h, 1), :] = i
        return carry

    lax.fori_loop(0, PEER_HEADS, head, 0)

    s1 = [s_scr[r] for r in range(PEER_TOPK)]
    s2 = [s_scr[PEER_TOPK + r] for r in range(PEER_TOPK)]
    i1 = [i_scr[r] for r in range(PEER_TOPK)]
    i2 = [i_scr[PEER_TOPK + r] for r in range(PEER_TOPK)]
    cand = [s1[a] + s2[b] for a, b in _PEER_PAIRS]
    cidx = [i1[a] * nk + i2[b] for a, b in _PEER_PAIRS]
    best_s, best_i = [], []
    for r in range(PEER_TOPK):
        m = functools.reduce(jnp.maximum, cand)
        found = jnp.zeros(m.shape, jnp.bool_)
        sel = jnp.zeros(m.shape, jnp.int32)
        for c in range(len(cand)):
            hit = cand[c] == m
            take = jnp.logical_and(hit, jnp.logical_not(found))
            found = jnp.logical_or(found, hit)
            sel = jnp.where(take, cidx[c], sel)
            cand[c] = jnp.where(take, -jnp.inf, cand[c])
        best_s.append(m)
        best_i.append(sel)
    e = [jnp.exp(s - best_s[0]) for s in best_s]
    tot = functools.reduce(lambda p, q: p + q, e)
    for r in range(PEER_TOPK):
        off_ref[r] = best_i[r] >> 1
        par_ref[r] = (best_i[r] & 1).astype(F32)
        gate_ref[r] = e[r] / tot


def _peer_route(qp, subkeys):
    t = qp.shape[0]
    tt = PEER_ROUTE_BLOCK
    out = pl.BlockSpec((PEER_TOPK, PEER_HEADS, tt), lambda i: (0, 0, i))
    shp = (PEER_TOPK, PEER_HEADS, t)
    off, par, gate = pl.pallas_call(
        _route_kernel,
        grid=(t // tt,),
        in_specs=[pl.BlockSpec((tt, qp.shape[1]), lambda i: (i, 0)),
                  pl.BlockSpec(subkeys.shape, lambda i: (0, 0, 0))],
        out_specs=[out, out, out],
        out_shape=[jax.ShapeDtypeStruct(shp, jnp.int32), jax.ShapeDtypeStruct(shp, F32),
                   jax.ShapeDtypeStruct(shp, F32)],
        scratch_shapes=[pltpu.VMEM((2 * PEER_TOPK, PEER_HEADS, tt), F32),
                        pltpu.VMEM((2 * PEER_TOPK, PEER_HEADS, tt), jnp.int32)],
        compiler_params=_cparams(("parallel",)),
        name="peer_route",
    )(qp, subkeys.astype(BF16))
    tok_major = lambda a: a.transpose(2, 1, 0).reshape(t, PEER_HEADS * PEER_TOPK)
    return tok_major(off).reshape(-1), tok_major(par), tok_major(gate)


NSEL = PEER_HEADS * PEER_TOPK
WROWS = NSEL * BF16_ROWS


def _peer_tables(tab):
    e, d = tab.shape
    c = d // LANES
    return tab.astype(BF16).reshape(e // 2, 2, c, LANES).transpose(0, 2, 1, 3).reshape(e // 2, 2 * c, LANES)


def _expand_consts():
    k = np.arange(WROWS) // BF16_ROWS
    r = np.arange(WROWS) % BF16_ROWS
    e = (k[None, :] == np.arange(NSEL)[:, None]).astype(np.float32)
    e2 = np.concatenate([e * (r % 2 == 0)[None, :], e * (r % 2 == 1)[None, :]], axis=0)
    return e, e2


def _gather_tiles(off_ref, tab_ref, w_scr, t):
    base = t * NSEL
    for k in range(NSEL):
        w_scr[k * BF16_ROWS:(k + 1) * BF16_ROWS, :] = tab_ref[off_ref[base + k]]


def _peer_up_kernel(off_ref, x_ref, par_ref, gate_ref, tab_ref, e_ref, et_ref, act_ref, w_scr, pe_scr, zz_scr):
    tt = PEER_BLOCK
    nt = (((1,), (1,)), ((), ()))
    par = par_ref[...]
    pe_scr[...] = jnp.dot(par.astype(BF16), e_ref[...], preferred_element_type=F32)
    row_r = lax.broadcasted_iota(jnp.int32, (SUBLANES, WROWS), 1) % BF16_ROWS
    chunk = lax.broadcasted_iota(jnp.int32, (SUBLANES, WROWS), 0)
    want_even = (row_r - 2 * chunk).astype(F32)

    def tok(t, carry):
        _gather_tiles(off_ref, tab_ref, w_scr, t)
        r = lax.dot_general(x_ref[t].astype(BF16), w_scr[...], nt, preferred_element_type=F32)
        mask = want_even == pe_scr[pl.ds(t, 1), :]
        zz_scr[pl.ds(pl.multiple_of(t * SUBLANES, SUBLANES), SUBLANES), :] = jnp.where(mask, r, 0.0)
        return carry

    lax.fori_loop(0, tt, tok, 0, unroll=8)
    hd8 = jnp.dot(zz_scr[...].astype(BF16), et_ref[...], preferred_element_type=F32)
    hidden = jnp.sum(hd8.reshape(tt, SUBLANES, NSEL), axis=1)
    act = gate_ref[...] * _gelu_tanh(hidden)
    act_ref[:, :NSEL] = act * (1.0 - par)
    act_ref[:, NSEL:] = act * par


def _peer_down_kernel(off_ref, act_ref, h_ref, tab_ref, e2_ref, lg_ref, lb_ref, o_ref, w_scr, ae_scr, f_scr):
    tt = PEER_BLOCK
    ae_scr[...] = jnp.dot(act_ref[...].astype(BF16), e2_ref[...], preferred_element_type=F32)
    row_r = lax.broadcasted_iota(jnp.int32, (SUBLANES, WROWS), 1) % BF16_ROWS
    chunk = lax.broadcasted_iota(jnp.int32, (SUBLANES, WROWS), 0)
    m_chunk = (row_r >> 1) == chunk

    def tok(t, carry):
        _gather_tiles(off_ref, tab_ref, w_scr, t)
        lhs = jnp.where(m_chunk, ae_scr[pl.ds(t, 1), :], 0.0).astype(BF16)
        f_scr[t] = jnp.dot(lhs, w_scr[...], preferred_element_type=F32)
        return carry

    lax.fori_loop(0, tt, tok, 0, unroll=8)
    v = DN_ALPHA * h_ref[...] + f_scr[...]
    inv_d = 1.0 / D_MODEL
    mu = jnp.sum(jnp.sum(v, axis=2, keepdims=True), axis=1, keepdims=True) * inv_d
    vc = v - mu
    var = jnp.sum(jnp.sum(vc * vc, axis=2, keepdims=True), axis=1, keepdims=True) * inv_d
    o_ref[...] = vc * lax.rsqrt(var + LN_EPS) * lg_ref[...] + lb_ref[...]


def _peer(h1, qp, subkeys, u_tab, v_tab, lg, lb):
    t, d = h1.shape
    tt = PEER_BLOCK
    c = d // LANES
    off, par, gate = _peer_route(qp, subkeys)
    e, e2 = _expand_consts()
    e_bf, et_bf, e2_bf = jnp.asarray(e, BF16), jnp.asarray(e.T, BF16), jnp.asarray(e2, BF16)
    utab, vtab = _peer_tables(u_tab), _peer_tables(v_tab)
    h3 = h1.reshape(t, c, LANES)
    smem = pl.BlockSpec((tt * NSEL,), lambda i: (i,), memory_space=pltpu.SMEM)
    resident = lambda a: pl.BlockSpec(a.shape, lambda i: tuple(0 for _ in a.shape), pipeline_mode=pl.Buffered(1))
    rows = lambda w: pl.BlockSpec((tt, w), lambda i: (i, 0))
    tiles = pl.BlockSpec((tt, c, LANES), lambda i: (i, 0, 0))
    act = pl.pallas_call(
        _peer_up_kernel,
        grid=(t // tt,),
        in_specs=[smem, tiles, rows(NSEL), rows(NSEL), resident(utab), resident(e_bf), resident(et_bf)],
        out_specs=rows(2 * NSEL),
        out_shape=jax.ShapeDtypeStruct((t, 2 * NSEL), F32),
        scratch_shapes=[pltpu.VMEM((WROWS, LANES), BF16), pltpu.VMEM((tt, WROWS), F32),
                        pltpu.VMEM((tt * SUBLANES, WROWS), F32)],
        compiler_params=_cparams(("parallel",)),
        name="peer_up",
    )(off, h3, par, gate, utab, e_bf, et_bf)
    lg3 = lg.astype(F32).reshape(1, c, LANES)
    lb3 = lb.astype(F32).reshape(1, c, LANES)
    out = pl.pallas_call(
        _peer_down_kernel,
        grid=(t // tt,),
        in_specs=[smem, rows(2 * NSEL), tiles, resident(vtab), resident(e2_bf), resident(lg3), resident(lb3)],
        out_specs=tiles,
        out_shape=jax.ShapeDtypeStruct((t, c, LANES), F32),
        scratch_shapes=[pltpu.VMEM((WROWS, LANES), BF16), pltpu.VMEM((tt, WROWS), F32),
                        pltpu.VMEM((tt, c, LANES), F32)],
        compiler_params=_cparams(("parallel",)),
        name="peer_down",
    )(off, act, h3, vtab, e2_bf, lg3, lb3)
    return out.reshape(t, d)


def kernel(x, positions, ln_in_g, ln_in_b, w_in, s5_a_re, s5_a_im, s5_log_dt, s5_b_re, s5_b_im, s5_c_re, s5_c_im, s5_d, s5_w_glu, m2_conv_w, m2_conv_b, m2_dt_bias, m2_a_log, m2_d, m2_norm_w, mla_q_norm, mla_w_uq, mla_kv_norm, mla_w_ukv, w_br_a, w_br_b, w_br_c, w_out, ln1_g, ln1_b, peer_w_q, peer_subkeys, peer_u, peer_v, ln2_g, ln2_b):
    b, s, d = x.shape
    t = b * s
    assert d == D_MODEL and s % ATT_BLOCK == 0 and s % SSD_CHUNK == 0 and t % ROW_BLOCK == 0
    c_s5, c_z, c_xbc, c_dt, c_q, c_kv, c_kr, _ = np.cumsum(IN_SPLITS).tolist()
    h = _layer_norm_in(x.reshape(t, d).astype(F32), ln_in_g.astype(F32), ln_in_b.astype(F32))
    for l in range(DEPTH):
        w = w_in[l].astype(F32)
        w_kr = w[:, c_kv:c_kr]
        w_small = jnp.concatenate(
            [w[:, c_dt:c_kv], w[:, c_xbc:c_dt], jnp.zeros((d, LANES // 2 - M2_HEADS), F32), w_kr, _rot_cols(w_kr)],
            axis=1)
        u = _matmul(h, w[:, :c_s5].astype(BF16), F32, S5_WIDTH, "proj_s5")
        zx = _matmul(h, w[:, c_s5:c_xbc].astype(BF16), F32, (c_xbc - c_s5) // 2, "proj_ssd")
        small = _matmul(h, w_small.astype(BF16), F32, w_small.shape[1], "proj_small")
        gates = _matmul(h, w[:, c_kr:].astype(BF16), BF16, D_MODEL, "proj_gates")
        ya = _s5_branch(u, s, s5_a_re[l], s5_a_im[l], s5_log_dt[l], s5_b_re[l], s5_b_im[l],
                        s5_c_re[l], s5_c_im[l], s5_d[l], s5_w_glu[l])
        yb = _ssd_branch(zx, small, (MLA_Q_RANK + MLA_KV_RANK) // LANES, b, s, m2_conv_w[l], m2_conv_b[l],
                         m2_dt_bias[l], m2_a_log[l], m2_d[l], m2_norm_w[l])
        yc = _mla_branch(small, positions, b, s, mla_q_norm[l], mla_w_uq[l], mla_kv_norm[l], mla_w_ukv[l])
        h1, qp = _merge(h, ya, yb, yc, gates, w_br_a[l], w_br_b[l], w_br_c[l], w_out[l], ln1_g[l], ln1_b[l],
                        peer_w_q[l])
        h = _peer(h1, qp, peer_subkeys[l], peer_u[l], peer_v[l], ln2_g[l], ln2_b[l])
    return h.reshape(b, s, d).astype(x.dtype)
```

```python
import functools
import math

import numpy as np
import jax
import jax.numpy as jnp
from jax import lax
from jax.experimental import pallas as pl
from jax.experimental.pallas import tpu as pltpu

F32 = jnp.float32
BF16 = jnp.bfloat16
HIGHEST = lax.Precision.HIGHEST

D_MODEL = 1024
DEPTH = 2
S5_GROUP = 16
S5_GROUPS = 24
S5_WIDTH = S5_GROUP * S5_GROUPS
S5_STATE = 64
M2_HEADDIM = 64
M2_INNER = 512
M2_HEADS = M2_INNER // M2_HEADDIM
M2_GROUPS = 2
M2_STATE = 128
M2_CONV = 4
M2_CONV_DIM = M2_INNER + 2 * M2_GROUPS * M2_STATE
MLA_HEADS = 8
MLA_Q_RANK = 256
MLA_KV_RANK = 128
MLA_NOPE = 64
MLA_ROPE = 32
MLA_V = 64
MLA_WIDTH = MLA_HEADS * MLA_V
ROPE_THETA = 10000.0
PEER_HEADS = 8
PEER_NKEYS = 128
PEER_EXPERTS = PEER_NKEYS * PEER_NKEYS
PEER_KEY_DIM = 256
PEER_TOPK = 16
N_BRANCHES = 3
IN_SPLITS = (S5_WIDTH, M2_INNER, M2_CONV_DIM, M2_HEADS, MLA_Q_RANK, MLA_KV_RANK, MLA_ROPE, N_BRANCHES * D_MODEL)
DN_ALPHA = (2 * DEPTH) ** 0.25
LN_EPS = 1e-5
RMS_EPS = 1e-6

LANES = 128
SUBLANES = 8
BF16_ROWS = 16
VMEM_LIMIT = 56 * 1024 * 1024

S5_CHUNK = 32
SSD_CHUNK = 128
ATT_Q_BLOCK = 512
ATT_KV_BLOCK = 512
ROW_BLOCK = 512
PEER_ROUTE_BLOCK = 256
PEER_BLOCK = 128


def _cparams(sem):
    return pltpu.CompilerParams(dimension_semantics=sem, vmem_limit_bytes=VMEM_LIMIT)


def _ln_rows(v, g, b):
    mu = jnp.mean(v, -1, keepdims=True)
    vc = v - mu
    var = jnp.mean(vc * vc, -1, keepdims=True)
    return vc * lax.rsqrt(var + LN_EPS) * g + b


def _gelu_tanh(v):
    return 0.5 * v * (1.0 + jnp.tanh(math.sqrt(2.0 / math.pi) * (v + 0.044715 * (v * v * v))))


def _sigmoid(v):
    return 1.0 / (1.0 + jnp.exp(-v))


def _silu(v):
    return v * _sigmoid(v)


def _ln_kernel(x_ref, g_ref, b_ref, o_ref):
    o_ref[...] = _ln_rows(x_ref[...], g_ref[...], b_ref[...])


def _layer_norm_in(xt, g, b):
    t, d = xt.shape
    return pl.pallas_call(
        _ln_kernel,
        grid=(t // ROW_BLOCK,),
        in_specs=[pl.BlockSpec((ROW_BLOCK, d), lambda i: (i, 0)),
                  pl.BlockSpec((1, d), lambda i: (0, 0)),
                  pl.BlockSpec((1, d), lambda i: (0, 0))],
        out_specs=pl.BlockSpec((ROW_BLOCK, d), lambda i: (i, 0)),
        out_shape=jax.ShapeDtypeStruct((t, d), F32),
        compiler_params=_cparams(("parallel",)),
        name="ln_in",
    )(xt, g.reshape(1, d), b.reshape(1, d))


def _mm_kernel(x_ref, w_ref, o_ref):
    o_ref[...] = jnp.dot(x_ref[...].astype(BF16), w_ref[...], preferred_element_type=F32).astype(o_ref.dtype)


def _matmul(x, w, out_dtype, tn, name):
    t, k = x.shape
    n = w.shape[1]
    assert n % tn == 0 and t % ROW_BLOCK == 0
    return pl.pallas_call(
        _mm_kernel,
        grid=(t // ROW_BLOCK, n // tn),
        in_specs=[pl.BlockSpec((ROW_BLOCK, k), lambda i, j: (i, 0)),
                  pl.BlockSpec((k, tn), lambda i, j: (0, j))],
        out_specs=pl.BlockSpec((ROW_BLOCK, tn), lambda i, j: (i, j)),
        out_shape=jax.ShapeDtypeStruct((t, n), out_dtype),
        compiler_params=_cparams(("parallel", "arbitrary")),
        name=name,
    )(x, w)


def _s5_operators(a_re, a_im, log_dt, b_re, b_im, c_re, c_im, chunk, n_steps):
    lr, li = a_re.astype(F32), a_im.astype(F32)
    dt = jnp.exp(log_dt.astype(F32))[:, None]

    def apow(n):
        n = jnp.asarray(n, F32)[..., None, None]
        mag = jnp.exp(lr * dt * n)
        return mag * jnp.cos(li * dt * n), mag * jnp.sin(li * dt * n)

    abar_re, abar_im = apow(1.0)
    nr, ni = abar_re - 1.0, abar_im
    den = lr * lr + li * li
    coef_re = ((nr * lr + ni * li) / den)[..., None]
    coef_im = ((ni * lr - nr * li) / den)[..., None]
    br, bi = b_re.astype(F32), b_im.astype(F32)
    bbar_re = coef_re * br - coef_im * bi
    bbar_im = coef_re * bi + coef_im * br
    cr, ci = c_re.astype(F32), c_im.astype(F32)

    lags = jnp.arange(chunk + 1)
    pr, pi = apow(lags)
    cp_re = cr[None] * pr[:, :, None, :] - ci[None] * pi[:, :, None, :]
    cp_im = cr[None] * pi[:, :, None, :] + ci[None] * pr[:, :, None, :]
    g = lr.shape[0]
    w = chunk * S5_GROUP
    kern = (jnp.einsum('jghp,gpk->gkjh', cp_re[:chunk], bbar_re, precision=HIGHEST)
            - jnp.einsum('jghp,gpk->gkjh', cp_im[:chunk], bbar_im, precision=HIGHEST))
    kern = kern.reshape(g, S5_GROUP, w)
    tt = jnp.arange(chunk)
    tmat = jnp.stack([jnp.pad(kern[:, :, :w - s * S5_GROUP], ((0, 0), (0, 0), (s * S5_GROUP, 0)))
                      for s in range(chunk)], axis=1).reshape(g, w, w)

    rev = chunk - 1 - tt
    bm_re = pr[rev][:, :, :, None] * bbar_re[None] - pi[rev][:, :, :, None] * bbar_im[None]
    bm_im = pr[rev][:, :, :, None] * bbar_im[None] + pi[rev][:, :, :, None] * bbar_re[None]
    bmat = jnp.concatenate([bm_re, bm_im], axis=2)
    bmat = bmat.transpose(1, 0, 3, 2).reshape(g, w, 2 * S5_STATE)

    cm = jnp.concatenate([cp_re[1:], -cp_im[1:]], axis=3)
    cmat = cm.transpose(1, 3, 0, 2).reshape(g, 2 * S5_STATE, w)

    steps = chunk * (2 ** jnp.arange(n_steps))
    sr, si = apow(steps)
    par = jnp.concatenate([sr, sr], axis=-1)
    pai = jnp.concatenate([-si, si], axis=-1)
    scan = jnp.stack([par, pai], axis=2).transpose(1, 0, 2, 3)
    return tmat.astype(BF16), bmat.astype(BF16), cmat.astype(BF16), scan


def _s5_kernel(u_ref, t_ref, b_ref, c_ref, scan_ref, y_ref, *, chunks_per_seq, n_steps):
    u = u_ref[0]
    nc = u.shape[0]
    z = jnp.dot(u, b_ref[0], preferred_element_type=F32)
    row = lax.broadcasted_iota(jnp.int32, (nc, 1), 0) % chunks_per_seq
    for k in range(n_steps):
        sh = 1 << k
        zs = jnp.where(row >= sh, pltpu.roll(z, sh, axis=0), 0.0)
        zsw = pltpu.roll(zs, S5_STATE, axis=1)
        z = z + scan_ref[0, k, 0:1, :] * zs + scan_ref[0, k, 1:2, :] * zsw
    s0 = jnp.where(row >= 1, pltpu.roll(z, 1, axis=0), 0.0)
    y = jnp.dot(u, t_ref[0], preferred_element_type=F32)
    y = y + jnp.dot(s0.astype(BF16), c_ref[0], preferred_element_type=F32)
    y_ref[0] = y


def _s5_scan(u_g, tmat, bmat, cmat, scan, chunks_per_seq):
    g, nc, w = u_g.shape
    n_steps = scan.shape[1]
    return pl.pallas_call(
        functools.partial(_s5_kernel, chunks_per_seq=chunks_per_seq, n_steps=n_steps),
        grid=(g,),
        in_specs=[pl.BlockSpec((1, nc, w), lambda i: (i, 0, 0)),
                  pl.BlockSpec((1, w, w), lambda i: (i, 0, 0)),
                  pl.BlockSpec((1, w, 2 * S5_STATE), lambda i: (i, 0, 0)),
                  pl.BlockSpec((1, 2 * S5_STATE, w), lambda i: (i, 0, 0)),
                  pl.BlockSpec((1, n_steps, 2, 2 * S5_STATE), lambda i: (i, 0, 0, 0))],
        out_specs=pl.BlockSpec((1, nc, w), lambda i: (i, 0, 0)),
        out_shape=jax.ShapeDtypeStruct((g, nc, w), F32),
        compiler_params=_cparams(("parallel",)),
        name="s5_scan",
    )(u_g, tmat, bmat, cmat, scan)


def _s5_glu_kernel(y_ref, u_ref, d_ref, w_ref, o_ref):
    v = y_ref[...] + d_ref[...] * u_ref[...]
    z = _gelu_tanh(v)
    gate = _sigmoid(jnp.dot(z.astype(BF16), w_ref[...], preferred_element_type=F32))
    o_ref[...] = (z * gate).astype(o_ref.dtype)


def _s5_glu(y, u, d, w_glu):
    t, w = y.shape
    row = pl.BlockSpec((ROW_BLOCK, w), lambda i: (i, 0))
    return pl.pallas_call(
        _s5_glu_kernel,
        grid=(t // ROW_BLOCK,),
        in_specs=[row, row, pl.BlockSpec((1, w), lambda i: (0, 0)), pl.BlockSpec((w, w), lambda i: (0, 0))],
        out_specs=row,
        out_shape=jax.ShapeDtypeStruct((t, w), BF16),
        compiler_params=_cparams(("parallel",)),
        name="s5_glu",
    )(y, u, d.reshape(1, w).astype(F32), w_glu.astype(BF16))


def _s5_branch(u, seq, a_re, a_im, log_dt, b_re, b_im, c_re, c_im, d, w_glu):
    t = u.shape[0]
    chunk = S5_CHUNK
    cps = seq // chunk
    n_steps = max(1, (cps - 1).bit_length())
    tmat, bmat, cmat, scan = _s5_operators(a_re, a_im, log_dt, b_re, b_im, c_re, c_im, chunk, n_steps)
    nc = t // chunk
    u_g = (u.reshape(nc, chunk, S5_GROUPS, S5_GROUP).transpose(2, 0, 1, 3)
           .reshape(S5_GROUPS, nc, chunk * S5_GROUP).astype(BF16))
    y_g = _s5_scan(u_g, tmat, bmat, cmat, scan, cps)
    y = (y_g.reshape(S5_GROUPS, nc, chunk, S5_GROUP).transpose(1, 2, 0, 3).reshape(t, S5_WIDTH))
    return _s5_glu(y, u, d, w_glu)


def _softplus(v):
    return jnp.maximum(v, 0.0) + jnp.log1p(jnp.exp(-jnp.abs(v)))


def _ssd_kernel(zx_ref, dt_ref, dtt_ref, cw_ref, cb_ref, dtb_ref, dtbt_ref, alog_ref, alogt_ref,
                dskip_ref, nw_ref, o_ref, ext_ref, state_ref):
    L = SSD_CHUNK
    n = M2_STATE
    p = M2_HEADDIM

    @pl.when(pl.program_id(1) == 0)
    def _():
        ext_ref[0:SUBLANES, :] = jnp.zeros((SUBLANES, M2_CONV_DIM), F32)
        state_ref[...] = jnp.zeros_like(state_ref)

    z = zx_ref[:, :M2_INNER]
    xbc = zx_ref[:, M2_INNER:]
    ext_ref[SUBLANES:SUBLANES + L, :] = xbc
    conv = cb_ref[...] + cw_ref[M2_CONV - 1:M2_CONV, :] * xbc
    for k in range(M2_CONV - 1):
        lo = SUBLANES - (M2_CONV - 1) + k
        conv = conv + cw_ref[k:k + 1, :] * ext_ref[lo:lo + L, :]
    ext_ref[0:SUBLANES, :] = ext_ref[L:L + SUBLANES, :]
    xc = _silu(conv)
    xs = xc[:, :M2_INNER]
    bm = xc[:, M2_INNER:M2_INNER + M2_GROUPS * n]
    cm = xc[:, M2_INNER + M2_GROUPS * n:]

    dt = _softplus(dt_ref[...] + dtb_ref[...])
    a = -jnp.exp(alog_ref[...]) * dt
    a_t = -jnp.exp(alogt_ref[...]) * _softplus(dtt_ref[...] + dtbt_ref[...])
    ri = lax.broadcasted_iota(jnp.int32, (L, L), 0)
    ci = lax.broadcasted_iota(jnp.int32, (L, L), 1)
    causal = ci <= ri
    tril = causal.astype(F32)
    triu = (ri <= ci).astype(F32)
    acum_col = jnp.dot(tril, a, preferred_element_type=F32, precision=HIGHEST)
    acum_row = jnp.dot(a_t, triu, preferred_element_type=F32, precision=HIGHEST)

    ys = []
    rep = M2_HEADS // M2_GROUPS
    for g in range(M2_GROUPS):
        bg = bm[:, g * n:(g + 1) * n]
        cgb = cm[:, g * n:(g + 1) * n].astype(BF16)
        scores = lax.dot_general(cgb, bg.astype(BF16), (((1,), (1,)), ((), ())), preferred_element_type=F32)
        bg_t = bg.T.astype(BF16)
        for hh in range(rep):
            h = g * rep + hh
            ac = acum_col[:, h:h + 1]
            ar = acum_row[h:h + 1, :]
            decay = jnp.exp(jnp.where(causal, ac - ar, -jnp.inf))
            xh = xs[:, h * p:(h + 1) * p]
            xdt = xh * dt[:, h:h + 1]
            y = jnp.dot((scores * decay).astype(BF16), xdt.astype(BF16), preferred_element_type=F32)
            st = state_ref[h]
            y = y + jnp.dot(cgb, st.astype(BF16), preferred_element_type=F32) * jnp.exp(ac)
            y = y + dskip_ref[:, h * p:(h + 1) * p] * xh
            a_last = acum_col[L - 1:L, h:h + 1]
            xw = (xdt * jnp.exp(a_last - ac)).astype(BF16)
            state_ref[h] = jnp.exp(a_last) * st + jnp.dot(bg_t, xw, preferred_element_type=F32)
            ys.append(y)
    y = jnp.concatenate(ys, axis=1) * _silu(z)
    gw = M2_INNER // M2_GROUPS
    outs = []
    for g in range(M2_GROUPS):
        yg = y[:, g * gw:(g + 1) * gw]
        outs.append(yg * lax.rsqrt(jnp.mean(yg * yg, -1, keepdims=True) + RMS_EPS))
    o_ref[...] = (jnp.concatenate(outs, axis=1) * nw_ref[...]).astype(o_ref.dtype)


def _ssd_branch(zx, small, dt_col_block, batch, seq, conv_w, conv_b, dt_bias, a_log, d, norm_w):
    t = zx.shape[0]
    L = SSD_CHUNK
    ncs = seq // L
    hh = M2_HEADS
    dt_t = small[:, dt_col_block * LANES:dt_col_block * LANES + hh].T
    lane_pad = lambda v: jnp.zeros((1, LANES), F32).at[0, :hh].set(v.astype(F32))
    col = lambda v: v.astype(F32).reshape(hh, 1)
    dskip = jnp.repeat(d.astype(F32), M2_HEADDIM).reshape(1, M2_INNER)
    wz = M2_INNER + M2_CONV_DIM
    const = lambda shape: pl.BlockSpec(shape, lambda b, c: tuple(0 for _ in shape))
    return pl.pallas_call(
        _ssd_kernel,
        grid=(batch, ncs),
        in_specs=[pl.BlockSpec((L, wz), lambda b, c: (b * ncs + c, 0)),
                  pl.BlockSpec((L, LANES), lambda b, c: (b * ncs + c, dt_col_block)),
                  pl.BlockSpec((hh, L), lambda b, c: (0, b * ncs + c)),
                  const((M2_CONV, M2_CONV_DIM)), const((1, M2_CONV_DIM)),
                  const((1, LANES)), const((hh, 1)), const((1, LANES)), const((hh, 1)),
                  const((1, M2_INNER)), const((1, M2_INNER))],
        out_specs=pl.BlockSpec((L, M2_INNER), lambda b, c: (b * ncs + c, 0)),
        out_shape=jax.ShapeDtypeStruct((t, M2_INNER), BF16),
        scratch_shapes=[pltpu.VMEM((L + SUBLANES, M2_CONV_DIM), F32),
                        pltpu.VMEM((hh, M2_STATE, M2_HEADDIM), F32)],
        compiler_params=_cparams(("parallel", "arbitrary")),
        name="ssd",
    )(zx, small, dt_t, conv_w.astype(F32), conv_b.astype(F32).reshape(1, -1),
      lane_pad(dt_bias), col(dt_bias), lane_pad(a_log), col(a_log), dskip,
      norm_w.astype(F32).reshape(1, -1))


def _rot_cols(w):
    half = MLA_ROPE // 2
    return jnp.concatenate([-w[..., half:], w[..., :half]], axis=-1)


def _mla_prep_kernel(cq_ref, ckv_ref, krd_ref, ck_ref, sk_ref, qn_ref, kvn_ref, wq_ref, wkv_ref,
                     q_out, k_out, v_out):
    cq = cq_ref[...]
    qn = cq * lax.rsqrt(jnp.mean(cq * cq, -1, keepdims=True) + RMS_EPS) * qn_ref[...]
    q = jnp.dot(qn.astype(BF16), wq_ref[...], preferred_element_type=F32)
    ckv = ckv_ref[...]
    kvn = ckv * lax.rsqrt(jnp.mean(ckv * ckv, -1, keepdims=True) + RMS_EPS) * kvn_ref[...]
    kv = jnp.dot(kvn.astype(BF16), wkv_ref[...], preferred_element_type=F32)
    ck, sk = ck_ref[...], sk_ref[...]
    krd = krd_ref[...]
    shift = LANES - MLA_ROPE
    kr = krd * ck + pltpu.roll(krd, shift, axis=1) * sk
    scale = (MLA_NOPE + MLA_ROPE) ** -0.5
    lane = lax.broadcasted_iota(jnp.int32, (1, LANES), 1)
    cq_t = scale * (ck + (lane < MLA_NOPE).astype(F32))
    sq_t = scale * sk
    ones_col = (lane == MLA_V).astype(F32)
    for h in range(MLA_HEADS):
        qh = q[:, h * LANES:(h + 1) * LANES]
        q_out[h] = (qh * cq_t + pltpu.roll(qh, shift, axis=1) * sq_t).astype(BF16)
        k_out[h] = (kv[:, 2 * h * LANES:(2 * h + 1) * LANES] + kr).astype(BF16)
        v_out[h] = (kv[:, (2 * h + 1) * LANES:(2 * h + 2) * LANES] + ones_col).astype(BF16)


def _attn_kernel(q_ref, k_ref, v_ref, o_ref, m_ref, acc_ref):
    qi = pl.program_id(2)
    tq, tk = ATT_Q_BLOCK, ATT_KV_BLOCK
    per_q = tq // tk
    nt = (((1,), (1,)), ((), ()))
    m_ref[...] = jnp.full(m_ref.shape, -jnp.inf, F32)
    acc_ref[...] = jnp.zeros(acc_ref.shape, F32)

    def step(j, diag):
        start = pl.multiple_of(j * tk, tk)
        for hh in range(2):
            k = k_ref[hh, pl.ds(start, tk), :]
            v = v_ref[hh, pl.ds(start, tk), :]
            s = lax.dot_general(q_ref[hh], k, nt, preferred_element_type=F32)
            if diag is not None:
                ri = lax.broadcasted_iota(jnp.int32, (tq, tk), 0)
                ci = lax.broadcasted_iota(jnp.int32, (tq, tk), 1) + diag * tk
                s = jnp.where(ci <= ri, s, -jnp.inf)
            m_old = m_ref[hh]
            m_new = jnp.maximum(m_old, jnp.max(s, axis=1, keepdims=True))
            pexp = jnp.exp(s - m_new)
            acc_ref[hh] = acc_ref[hh] * jnp.exp(m_old - m_new) + jnp.dot(
                pexp.astype(BF16), v, preferred_element_type=F32)
            m_ref[hh] = m_new

    def body(j, carry):
        step(j, None)
        return carry

    lax.fori_loop(0, qi * per_q, body, 0)
    for dg in range(per_q):
        step(qi * per_q + dg, dg)
    outs = []
    for hh in range(2):
        acc = acc_ref[hh]
        outs.append(acc[:, :MLA_V] / acc[:, MLA_V:MLA_V + 1])
    o_ref[...] = jnp.concatenate(outs, axis=1).astype(o_ref.dtype)


def _mla_branch(small, positions, batch, seq, q_norm, w_uq, kv_norm, w_ukv):
    t = small.shape[0]
    hh = MLA_HEADS
    inv_freq = 1.0 / (ROPE_THETA ** (jnp.arange(0, MLA_ROPE, 2, dtype=F32) / MLA_ROPE))
    ang = positions.astype(F32).reshape(t, 1) * inv_freq
    cos2 = jnp.concatenate([jnp.cos(ang), jnp.cos(ang)], axis=-1)
    sin2 = jnp.concatenate([jnp.sin(ang), jnp.sin(ang)], axis=-1)
    ck = jnp.zeros((t, LANES), F32).at[:, MLA_NOPE:MLA_NOPE + MLA_ROPE].set(cos2)
    sk = jnp.zeros((t, LANES), F32).at[:, MLA_NOPE:MLA_NOPE + MLA_ROPE].set(sin2)
    wq = w_uq.astype(F32).reshape(MLA_Q_RANK, hh, MLA_NOPE + MLA_ROPE)
    wq = jnp.concatenate([wq, _rot_cols(wq[..., MLA_NOPE:])], axis=-1).reshape(MLA_Q_RANK, hh * LANES)
    wkv = w_ukv.astype(F32).reshape(MLA_KV_RANK, hh, MLA_NOPE + MLA_V)
    zpad = jnp.zeros((MLA_KV_RANK, hh, LANES - MLA_NOPE), F32)
    wkv = jnp.concatenate([wkv[..., :MLA_NOPE], zpad, wkv[..., MLA_NOPE:], zpad], axis=-1)
    wkv = wkv.reshape(MLA_KV_RANK, hh * 2 * LANES)
    tm = ROW_BLOCK
    const = lambda shape: pl.BlockSpec(shape, lambda i: tuple(0 for _ in shape))
    head_out = pl.BlockSpec((hh, tm, LANES), lambda i: (0, i, 0))
    q, k, v = pl.pallas_call(
        _mla_prep_kernel,
        grid=(t // tm,),
        in_specs=[pl.BlockSpec((tm, MLA_Q_RANK), lambda i: (i, 0)),
                  pl.BlockSpec((tm, MLA_KV_RANK), lambda i: (i, MLA_Q_RANK // MLA_KV_RANK)),
                  pl.BlockSpec((tm, LANES), lambda i: (i, (MLA_Q_RANK + MLA_KV_RANK) // LANES)),
                  pl.BlockSpec((tm, LANES), lambda i: (i, 0)),
                  pl.BlockSpec((tm, LANES), lambda i: (i, 0)),
                  const((1, MLA_Q_RANK)), const((1, MLA_KV_RANK)),
                  const((MLA_Q_RANK, hh * LANES)), const((MLA_KV_RANK, hh * 2 * LANES))],
        out_specs=[head_out, head_out, head_out],
        out_shape=[jax.ShapeDtypeStruct((hh, t, LANES), BF16)] * 3,
        compiler_params=_cparams(("parallel",)),
        name="mla_prep",
    )(small, small, small, ck, sk, q_norm.astype(F32).reshape(1, -1), kv_norm.astype(F32).reshape(1, -1),
      wq.astype(BF16), wkv.astype(BF16))
    blk = ATT_Q_BLOCK
    nq = seq // blk
    return pl.pallas_call(
        _attn_kernel,
        grid=(batch, hh // 2, nq),
        in_specs=[pl.BlockSpec((2, blk, LANES), lambda b, hp, i: (hp, b * nq + i, 0)),
                  pl.BlockSpec((2, seq, LANES), lambda b, hp, i: (hp, b, 0)),
                  pl.BlockSpec((2, seq, LANES), lambda b, hp, i: (hp, b, 0))],
        out_specs=pl.BlockSpec((blk, LANES), lambda b, hp, i: (b * nq + i, hp)),
        out_shape=jax.ShapeDtypeStruct((t, MLA_WIDTH), BF16),
        scratch_shapes=[pltpu.VMEM((2, blk, 1), F32), pltpu.VMEM((2, blk, LANES), F32)],
        compiler_params=_cparams(("parallel", "parallel", "arbitrary")),
        name="mla_attn",
    )(q, k, v)


def _merge_kernel(h_ref, ya_ref, yb_ref, yc_ref, g_ref, wa_ref, wb_ref, wc_ref, wo_ref, lg_ref, lb_ref,
                  wq_ref, h1_ref, qp_ref):
    d = D_MODEL
    gates = _sigmoid(g_ref[...].astype(F32))
    merged = (gates[:, :d] * jnp.dot(ya_ref[...], wa_ref[...], preferred_element_type=F32)
              + gates[:, d:2 * d] * jnp.dot(yb_ref[...], wb_ref[...], preferred_element_type=F32)
              + gates[:, 2 * d:] * jnp.dot(yc_ref[...], wc_ref[...], preferred_element_type=F32))
    mix = jnp.dot(merged.astype(BF16), wo_ref[...], preferred_element_type=F32)
    h1 = _ln_rows(DN_ALPHA * h_ref[...] + mix, lg_ref[...], lb_ref[...])
    h1_ref[...] = h1
    qp_ref[...] = jnp.dot(h1.astype(BF16), wq_ref[...], preferred_element_type=F32).astype(qp_ref.dtype)


def _merge(h, ya, yb, yc, gates, wa, wb, wc, wo, lg, lb, wq):
    t, d = h.shape
    tm = ROW_BLOCK
    row = lambda w: pl.BlockSpec((tm, w), lambda i: (i, 0))
    const = lambda a: pl.BlockSpec(a.shape, lambda i: (0, 0))
    ws = [wa.astype(BF16), wb.astype(BF16), wc.astype(BF16), wo.astype(BF16),
          lg.astype(F32).reshape(1, d), lb.astype(F32).reshape(1, d), wq.astype(BF16)]
    nq = wq.shape[1]
    return pl.pallas_call(
        _merge_kernel,
        grid=(t // tm,),
        in_specs=[row(d), row(ya.shape[1]), row(yb.shape[1]), row(yc.shape[1]), row(gates.shape[1])]
                 + [const(a) for a in ws],
        out_specs=[row(d), row(nq)],
        out_shape=[jax.ShapeDtypeStruct((t, d), F32), jax.ShapeDtypeStruct((t, nq), BF16)],
        compiler_params=_cparams(("parallel",)),
        name="merge",
    )(h, ya, yb, yc, gates, *ws)


_PEER_PAIRS = tuple((a, b) for a in range(PEER_TOPK) for b in range(PEER_TOPK) if (a + 1) * (b + 1) <= PEER_TOPK)


def _route_kernel(qp_ref, sk_ref, off_ref, par_ref, gate_ref, s_scr, i_scr):
    tt = PEER_ROUTE_BLOCK
    nk = PEER_NKEYS
    kd = PEER_KEY_DIM // 2
    nt = (((1,), (1,)), ((), ()))
    key_iota = lax.broadcasted_iota(jnp.int32, (nk, tt), 0)

    def head(h, carry):
        vals = []
        for j in range(2):
            q = qp_ref[:, pl.ds(pl.multiple_of((2 * h + j) * kd, kd), kd)]
            vals.append(lax.dot_general(sk_ref[j], q, nt, preferred_element_type=F32))
        for r in range(PEER_TOPK):
            for j in range(2):
                m = jnp.max(vals[j], axis=0, keepdims=True)
                i = jnp.min(jnp.where(vals[j] == m, key_iota, nk), axis=0, keepdims=True)
                vals[j] = jnp.where(key_iota == i, -jnp.inf, vals[j])
                s_scr[j * PEER_TOPK + r, pl.ds(h, 1), :] = m
                i_scr[j * PEER_TOPK + r, pl.ds(h, 1), :] = i
        return carry

    lax.fori_loop(0, PEER_HEADS, head, 0)

    s1 = [s_scr[r] for r in range(PEER_TOPK)]
    s2 = [s_scr[PEER_TOPK + r] for r in range(PEER_TOPK)]
    i1 = [i_scr[r] for r in range(PEER_TOPK)]
    i2 = [i_scr[PEER_TOPK + r] for r in range(PEER_TOPK)]
    cand = [s1[a] + s2[b] for a, b in _PEER_PAIRS]
    cidx = [i1[a] * nk + i2[b] for a, b in _PEER_PAIRS]
    best_s, best_i = [], []
    for r in range(PEER_TOPK):
        m = functools.reduce(jnp.maximum, cand)
        found = jnp.zeros(m.shape, jnp.bool_)
        sel = jnp.zeros(m.shape, jnp.int32)
        for c in range(len(cand)):
            hit = cand[c] == m
            take = jnp.logical_and(hit, jnp.logical_not(found))
            found = jnp.logical_or(found, hit)
            sel = jnp.where(take, cidx[c], sel)
            cand[c] = jnp.where(take, -jnp.inf, cand[c])
        best_s.append(m)
        best_i.append(sel)
    e = [jnp.exp(s - best_s[0]) for s in best_s]
    tot = functools.reduce(lambda p, q: p + q, e)
    sel_all = jnp.concatenate(best_i, axis=0)
    off_ref[...] = (sel_all >> 1).T
    par_ref[...] = (sel_all & 1).astype(F32).T
    gate_ref[...] = (jnp.concatenate(e, axis=0) / jnp.concatenate([tot] * PEER_TOPK, axis=0)).T


def _peer_route(qp, subkeys):
    t = qp.shape[0]
    tt = PEER_ROUTE_BLOCK
    out = pl.BlockSpec((tt, PEER_TOPK * PEER_HEADS), lambda i: (i, 0))
    shp = (t, PEER_TOPK * PEER_HEADS)
    off, par, gate = pl.pallas_call(
        _route_kernel,
        grid=(t // tt,),
        in_specs=[pl.BlockSpec((tt, qp.shape[1]), lambda i: (i, 0)),
                  pl.BlockSpec(subkeys.shape, lambda i: (0, 0, 0))],
        out_specs=[out, out, out],
        out_shape=[jax.ShapeDtypeStruct(shp, jnp.int32), jax.ShapeDtypeStruct(shp, F32),
                   jax.ShapeDtypeStruct(shp, F32)],
        scratch_shapes=[pltpu.VMEM((2 * PEER_TOPK, PEER_HEADS, tt), F32),
                        pltpu.VMEM((2 * PEER_TOPK, PEER_HEADS, tt), jnp.int32)],
        compiler_params=_cparams(("parallel",)),
        name="peer_route",
    )(qp, subkeys.astype(BF16))
    return off.reshape(-1), par, gate


NSEL = PEER_HEADS * PEER_TOPK
WROWS = NSEL * BF16_ROWS


def _peer_tables(tab):
    e, d = tab.shape
    c = d // LANES
    return tab.astype(BF16).reshape(e // 2, 2, c, LANES).transpose(0, 2, 1, 3).reshape(e // 2, 2 * c, LANES)


def _expand_consts():
    k = np.arange(WROWS) // BF16_ROWS
    r = np.arange(WROWS) % BF16_ROWS
    e = (k[None, :] == np.arange(NSEL)[:, None]).astype(np.float32)
    e2 = np.concatenate([e * (r % 2 == 0)[None, :], e * (r % 2 == 1)[None, :]], axis=0)
    return e, e2


def _gather_tiles(off_ref, tab_ref, w_scr, t):
    base = t * NSEL
    for k in range(NSEL):
        w_scr[k * BF16_ROWS:(k + 1) * BF16_ROWS, :] = tab_ref[off_ref[base + k]]


def _peer_up_kernel(off_ref, x_ref, par_ref, gate_ref, tab_ref, e_ref, et_ref, act_ref, w_scr, pe_scr, zz_scr):
    tt = PEER_BLOCK
    nt = (((1,), (1,)), ((), ()))
    par = par_ref[...]
    pe_scr[...] = jnp.dot(par.astype(BF16), e_ref[...], preferred_element_type=F32)
    row_r = lax.broadcasted_iota(jnp.int32, (SUBLANES, WROWS), 1) % BF16_ROWS
    chunk = lax.broadcasted_iota(jnp.int32, (SUBLANES, WROWS), 0)
    want_even = (row_r - 2 * chunk).astype(F32)

    def tok(t, carry):
        _gather_tiles(off_ref, tab_ref, w_scr, t)
        r = lax.dot_general(x_ref[t].astype(BF16), w_scr[...], nt, preferred_element_type=F32)
        mask = want_even == pe_scr[pl.ds(t, 1), :]
        zz_scr[pl.ds(pl.multiple_of(t * SUBLANES, SUBLANES), SUBLANES), :] = jnp.where(mask, r, 0.0)
        return carry

    lax.fori_loop(0, tt, tok, 0, unroll=8)
    hd8 = jnp.dot(zz_scr[...].astype(BF16), et_ref[...], preferred_element_type=F32)
    hidden = jnp.sum(hd8.reshape(tt, SUBLANES, NSEL), axis=1)
    act = gate_ref[...] * _gelu_tanh(hidden)
    act_ref[:, :NSEL] = act * (1.0 - par)
    act_ref[:, NSEL:] = act * par


def _peer_down_kernel(off_ref, act_ref, h_ref, tab_ref, e2_ref, lg_ref, lb_ref, o_ref, w_scr, ae_scr, f_scr):
    tt = PEER_BLOCK
    ae_scr[...] = jnp.dot(act_ref[...].astype(BF16), e2_ref[...], preferred_element_type=F32)
    row_r = lax.broadcasted_iota(jnp.int32, (SUBLANES, WROWS), 1) % BF16_ROWS
    chunk = lax.broadcasted_iota(jnp.int32, (SUBLANES, WROWS), 0)
    m_chunk = (row_r >> 1) == chunk

    def tok(t, carry):
        _gather_tiles(off_ref, tab_ref, w_scr, t)
        lhs = jnp.where(m_chunk, ae_scr[pl.ds(t, 1), :], 0.0).astype(BF16)
        f_scr[t] = jnp.dot(lhs, w_scr[...], preferred_element_type=F32)
        return carry

    lax.fori_loop(0, tt, tok, 0, unroll=8)
    v = DN_ALPHA * h_ref[...] + f_scr[...]
    inv_d = 1.0 / D_MODEL
    mu = jnp.sum(jnp.sum(v, axis=2, keepdims=True), axis=1, keepdims=True) * inv_d
    vc = v - mu
    var = jnp.sum(jnp.sum(vc * vc, axis=2, keepdims=True), axis=1, keepdims=True) * inv_d
    o_ref[...] = vc * lax.rsqrt(var + LN_EPS) * lg_ref[...] + lb_ref[...]


def _peer(h1, qp, subkeys, u_tab, v_tab, lg, lb):
    t, d = h1.shape
    tt = PEER_BLOCK
    c = d // LANES
    off, par, gate = _peer_route(qp, subkeys)
    e, e2 = _expand_consts()
    e_bf, et_bf, e2_bf = jnp.asarray(e, BF16), jnp.asarray(e.T, BF16), jnp.asarray(e2, BF16)
    utab, vtab = _peer_tables(u_tab), _peer_tables(v_tab)
    h3 = h1.reshape(t, c, LANES)
    smem = pl.BlockSpec((tt * NSEL,), lambda i: (i,), memory_space=pltpu.SMEM)
    resident = lambda a: pl.BlockSpec(a.shape, lambda i: tuple(0 for _ in a.shape), pipeline_mode=pl.Buffered(1))
    rows = lambda w: pl.BlockSpec((tt, w), lambda i: (i, 0))
    tiles = pl.BlockSpec((tt, c, LANES), lambda i: (i, 0, 0))
    act = pl.pallas_call(
        _peer_up_kernel,
        grid=(t // tt,),
        in_specs=[smem, tiles, rows(NSEL), rows(NSEL), resident(utab), resident(e_bf), resident(et_bf)],
        out_specs=rows(2 * NSEL),
        out_shape=jax.ShapeDtypeStruct((t, 2 * NSEL), F32),
        scratch_shapes=[pltpu.VMEM((WROWS, LANES), BF16), pltpu.VMEM((tt, WROWS), F32),
                        pltpu.VMEM((tt * SUBLANES, WROWS), F32)],
        compiler_params=_cparams(("parallel",)),
        name="peer_up",
    )(off, h3, par, gate, utab, e_bf, et_bf)
    lg3 = lg.astype(F32).reshape(1, c, LANES)
    lb3 = lb.astype(F32).reshape(1, c, LANES)
    out = pl.pallas_call(
        _peer_down_kernel,
        grid=(t // tt,),
        in_specs=[smem, rows(2 * NSEL), tiles, resident(vtab), resident(e2_bf), resident(lg3), resident(lb3)],
        out_specs=tiles,
        out_shape=jax.ShapeDtypeStruct((t, c, LANES), F32),
        scratch_shapes=[pltpu.VMEM((WROWS, LANES), BF16), pltpu.VMEM((tt, WROWS), F32),
                        pltpu.VMEM((tt, c, LANES), F32)],
        compiler_params=_cparams(("parallel",)),
        name="peer_down",
    )(off, act, h3, vtab, e2_bf, lg3, lb3)
    return out.reshape(t, d)


def kernel(x, positions, ln_in_g, ln_in_b, w_in, s5_a_re, s5_a_im, s5_log_dt, s5_b_re, s5_b_im, s5_c_re, s5_c_im, s5_d, s5_w_glu, m2_conv_w, m2_conv_b, m2_dt_bias, m2_a_log, m2_d, m2_norm_w, mla_q_norm, mla_w_uq, mla_kv_norm, mla_w_ukv, w_br_a, w_br_b, w_br_c, w_out, ln1_g, ln1_b, peer_w_q, peer_subkeys, peer_u, peer_v, ln2_g, ln2_b):
    b, s, d = x.shape
    t = b * s
    assert d == D_MODEL and s % ATT_Q_BLOCK == 0 and ATT_Q_BLOCK % ATT_KV_BLOCK == 0 and s % SSD_CHUNK == 0 and t % ROW_BLOCK == 0
    c_s5, c_z, c_xbc, c_dt, c_q, c_kv, c_kr, _ = np.cumsum(IN_SPLITS).tolist()
    h = _layer_norm_in(x.reshape(t, d).astype(F32), ln_in_g.astype(F32), ln_in_b.astype(F32))
    for l in range(DEPTH):
        w = w_in[l].astype(F32)
        w_kr = w[:, c_kv:c_kr]
        w_small = jnp.concatenate(
            [w[:, c_dt:c_kv], w[:, c_xbc:c_dt], jnp.zeros((d, LANES // 2 - M2_HEADS), F32), w_kr, _rot_cols(w_kr)],
            axis=1)
        u = _matmul(h, w[:, :c_s5].astype(BF16), F32, S5_WIDTH, "proj_s5")
        zx = _matmul(h, w[:, c_s5:c_xbc].astype(BF16), F32, (c_xbc - c_s5) // 2, "proj_ssd")
        small = _matmul(h, w_small.astype(BF16), F32, w_small.shape[1], "proj_small")
        gates = _matmul(h, w[:, c_kr:].astype(BF16), BF16, D_MODEL, "proj_gates")
        ya = _s5_branch(u, s, s5_a_re[l], s5_a_im[l], s5_log_dt[l], s5_b_re[l], s5_b_im[l],
                        s5_c_re[l], s5_c_im[l], s5_d[l], s5_w_glu[l])
        yb = _ssd_branch(zx, small, (MLA_Q_RANK + MLA_KV_RANK) // LANES, b, s, m2_conv_w[l], m2_conv_b[l],
                         m2_dt_bias[l], m2_a_log[l], m2_d[l], m2_norm_w[l])
        yc = _mla_branch(small, positions, b, s, mla_q_norm[l], mla_w_uq[l], mla_kv_norm[l], mla_w_ukv[l])
        h1, qp = _merge(h, ya, yb, yc, gates, w_br_a[l], w_br_b[l], w_br_c[l], w_out[l], ln1_g[l], ln1_b[l],
                        peer_w_q[l])
        h = _peer(h1, qp, peer_subkeys[l], peer_u[l], peer_v[l], ln2_g[l], ln2_b[l])
    return h.reshape(b, s, d).astype(x.dtype)
```

```python
import functools
import math

import numpy as np
import jax
import jax.numpy as jnp
from jax import lax
from jax.experimental import pallas as pl
from jax.experimental.pallas import tpu as pltpu

F32 = jnp.float32
BF16 = jnp.bfloat16
HIGHEST = lax.Precision.HIGHEST

D_MODEL = 1024
DEPTH = 2
S5_GROUP = 16
S5_GROUPS = 24
S5_WIDTH = S5_GROUP * S5_GROUPS
S5_STATE = 64
M2_HEADDIM = 64
M2_INNER = 512
M2_HEADS = M2_INNER // M2_HEADDIM
M2_GROUPS = 2
M2_STATE = 128
M2_CONV = 4
M2_CONV_DIM = M2_INNER + 2 * M2_GROUPS * M2_STATE
MLA_HEADS = 8
MLA_Q_RANK = 256
MLA_KV_RANK = 128
MLA_NOPE = 64
MLA_ROPE = 32
MLA_V = 64
MLA_WIDTH = MLA_HEADS * MLA_V
ROPE_THETA = 10000.0
PEER_HEADS = 8
PEER_NKEYS = 128
PEER_EXPERTS = PEER_NKEYS * PEER_NKEYS
PEER_KEY_DIM = 256
PEER_TOPK = 16
N_BRANCHES = 3
IN_SPLITS = (S5_WIDTH, M2_INNER, M2_CONV_DIM, M2_HEADS, MLA_Q_RANK, MLA_KV_RANK, MLA_ROPE, N_BRANCHES * D_MODEL)
DN_ALPHA = (2 * DEPTH) ** 0.25
LN_EPS = 1e-5
RMS_EPS = 1e-6

LANES = 128
SUBLANES = 8
BF16_ROWS = 16
VMEM_LIMIT = 56 * 1024 * 1024

S5_CHUNK = 32
SSD_CHUNK = 128
ATT_Q_BLOCK = 512
ATT_KV_BLOCK = 512
ROW_BLOCK = 512
PEER_ROUTE_BLOCK = 256
PEER_BLOCK = 128


def _cparams(sem):
    return pltpu.CompilerParams(dimension_semantics=sem, vmem_limit_bytes=VMEM_LIMIT)


def _ln_rows(v, g, b):
    mu = jnp.mean(v, -1, keepdims=True)
    vc = v - mu
    var = jnp.mean(vc * vc, -1, keepdims=True)
    return vc * lax.rsqrt(var + LN_EPS) * g + b


def _gelu_tanh(v):
    return 0.5 * v * (1.0 + jnp.tanh(math.sqrt(2.0 / math.pi) * (v + 0.044715 * (v * v * v))))


def _sigmoid(v):
    return 1.0 / (1.0 + jnp.exp(-v))


def _silu(v):
    return v * _sigmoid(v)


def _ln_kernel(x_ref, g_ref, b_ref, o_ref):
    o_ref[...] = _ln_rows(x_ref[...], g_ref[...], b_ref[...])


def _layer_norm_in(xt, g, b):
    t, d = xt.shape
    return pl.pallas_call(
        _ln_kernel,
        grid=(t // ROW_BLOCK,),
        in_specs=[pl.BlockSpec((ROW_BLOCK, d), lambda i: (i, 0)),
                  pl.BlockSpec((1, d), lambda i: (0, 0)),
                  pl.BlockSpec((1, d), lambda i: (0, 0))],
        out_specs=pl.BlockSpec((ROW_BLOCK, d), lambda i: (i, 0)),
        out_shape=jax.ShapeDtypeStruct((t, d), F32),
        compiler_params=_cparams(("parallel",)),
        name="ln_in",
    )(xt, g.reshape(1, d), b.reshape(1, d))


def _mm_kernel(x_ref, w_ref, o_ref):
    o_ref[...] = jnp.dot(x_ref[...].astype(BF16), w_ref[...], preferred_element_type=F32).astype(o_ref.dtype)


def _matmul(x, w, out_dtype, tn, name):
    t, k = x.shape
    n = w.shape[1]
    assert n % tn == 0 and t % ROW_BLOCK == 0
    return pl.pallas_call(
        _mm_kernel,
        grid=(t // ROW_BLOCK, n // tn),
        in_specs=[pl.BlockSpec((ROW_BLOCK, k), lambda i, j: (i, 0)),
                  pl.BlockSpec((k, tn), lambda i, j: (0, j))],
        out_specs=pl.BlockSpec((ROW_BLOCK, tn), lambda i, j: (i, j)),
        out_shape=jax.ShapeDtypeStruct((t, n), out_dtype),
        compiler_params=_cparams(("parallel", "arbitrary")),
        name=name,
    )(x, w)


def _s5_operators(a_re, a_im, log_dt, b_re, b_im, c_re, c_im, chunk, n_steps):
    lr, li = a_re.astype(F32), a_im.astype(F32)
    dt = jnp.exp(log_dt.astype(F32))[:, None]

    def apow(n):
        n = jnp.asarray(n, F32)[..., None, None]
        mag = jnp.exp(lr * dt * n)
        return mag * jnp.cos(li * dt * n), mag * jnp.sin(li * dt * n)

    abar_re, abar_im = apow(1.0)
    nr, ni = abar_re - 1.0, abar_im
    den = lr * lr + li * li
    coef_re = ((nr * lr + ni * li) / den)[..., None]
    coef_im = ((ni * lr - nr * li) / den)[..., None]
    br, bi = b_re.astype(F32), b_im.astype(F32)
    bbar_re = coef_re * br - coef_im * bi
    bbar_im = coef_re * bi + coef_im * br
    cr, ci = c_re.astype(F32), c_im.astype(F32)

    lags = jnp.arange(chunk + 1)
    pr, pi = apow(lags)
    cp_re = cr[None] * pr[:, :, None, :] - ci[None] * pi[:, :, None, :]
    cp_im = cr[None] * pi[:, :, None, :] + ci[None] * pr[:, :, None, :]
    g = lr.shape[0]
    w = chunk * S5_GROUP
    kern = (jnp.einsum('jghp,gpk->gkjh', cp_re[:chunk], bbar_re, precision=HIGHEST)
            - jnp.einsum('jghp,gpk->gkjh', cp_im[:chunk], bbar_im, precision=HIGHEST))
    kern = kern.reshape(g, S5_GROUP, w)
    tt = jnp.arange(chunk)
    tmat = jnp.stack([jnp.pad(kern[:, :, :w - s * S5_GROUP], ((0, 0), (0, 0), (s * S5_GROUP, 0)))
                      for s in range(chunk)], axis=1).reshape(g, w, w)

    rev = chunk - 1 - tt
    bm_re = pr[rev][:, :, :, None] * bbar_re[None] - pi[rev][:, :, :, None] * bbar_im[None]
    bm_im = pr[rev][:, :, :, None] * bbar_im[None] + pi[rev][:, :, :, None] * bbar_re[None]
    bmat = jnp.concatenate([bm_re, bm_im], axis=2)
    bmat = bmat.transpose(1, 0, 3, 2).reshape(g, w, 2 * S5_STATE)

    cm = jnp.concatenate([cp_re[1:], -cp_im[1:]], axis=3)
    cmat = cm.transpose(1, 3, 0, 2).reshape(g, 2 * S5_STATE, w)

    steps = chunk * (2 ** jnp.arange(n_steps))
    sr, si = apow(steps)
    par = jnp.concatenate([sr, sr], axis=-1)
    pai = jnp.concatenate([-si, si], axis=-1)
    scan = jnp.stack([par, pai], axis=2).transpose(1, 0, 2, 3)
    return tmat.astype(BF16), bmat.astype(BF16), cmat.astype(BF16), scan


def _s5_kernel(u_ref, t_ref, b_ref, c_ref, scan_ref, y_ref, *, chunks_per_seq, n_steps):
    u = u_ref[0]
    nc = u.shape[0]
    z = jnp.dot(u, b_ref[0], preferred_element_type=F32)
    row = lax.broadcasted_iota(jnp.int32, (nc, 1), 0) % chunks_per_seq
    for k in range(n_steps):
        sh = 1 << k
        zs = jnp.where(row >= sh, pltpu.roll(z, sh, axis=0), 0.0)
        zsw = pltpu.roll(zs, S5_STATE, axis=1)
        z = z + scan_ref[0, k, 0:1, :] * zs + scan_ref[0, k, 1:2, :] * zsw
    s0 = jnp.where(row >= 1, pltpu.roll(z, 1, axis=0), 0.0)
    y = jnp.dot(u, t_ref[0], preferred_element_type=F32)
    y = y + jnp.dot(s0.astype(BF16), c_ref[0], preferred_element_type=F32)
    y_ref[0] = y


def _s5_scan(u_g, tmat, bmat, cmat, scan, chunks_per_seq):
    g, nc, w = u_g.shape
    n_steps = scan.shape[1]
    return pl.pallas_call(
        functools.partial(_s5_kernel, chunks_per_seq=chunks_per_seq, n_steps=n_steps),
        grid=(g,),
        in_specs=[pl.BlockSpec((1, nc, w), lambda i: (i, 0, 0)),
                  pl.BlockSpec((1, w, w), lambda i: (i, 0, 0)),
                  pl.BlockSpec((1, w, 2 * S5_STATE), lambda i: (i, 0, 0)),
                  pl.BlockSpec((1, 2 * S5_STATE, w), lambda i: (i, 0, 0)),
                  pl.BlockSpec((1, n_steps, 2, 2 * S5_STATE), lambda i: (i, 0, 0, 0))],
        out_specs=pl.BlockSpec((1, nc, w), lambda i: (i, 0, 0)),
        out_shape=jax.ShapeDtypeStruct((g, nc, w), F32),
        compiler_params=_cparams(("parallel",)),
        name="s5_scan",
    )(u_g, tmat, bmat, cmat, scan)


def _s5_glu_kernel(y_ref, u_ref, d_ref, w_ref, o_ref):
    v = y_ref[...] + d_ref[...] * u_ref[...]
    z = _gelu_tanh(v)
    gate = _sigmoid(jnp.dot(z.astype(BF16), w_ref[...], preferred_element_type=F32))
    o_ref[...] = (z * gate).astype(o_ref.dtype)


def _s5_glu(y, u, d, w_glu):
    t, w = y.shape
    row = pl.BlockSpec((ROW_BLOCK, w), lambda i: (i, 0))
    return pl.pallas_call(
        _s5_glu_kernel,
        grid=(t // ROW_BLOCK,),
        in_specs=[row, row, pl.BlockSpec((1, w), lambda i: (0, 0)), pl.BlockSpec((w, w), lambda i: (0, 0))],
        out_specs=row,
        out_shape=jax.ShapeDtypeStruct((t, w), BF16),
        compiler_params=_cparams(("parallel",)),
        name="s5_glu",
    )(y, u, d.reshape(1, w).astype(F32), w_glu.astype(BF16))


def _s5_branch(u, seq, a_re, a_im, log_dt, b_re, b_im, c_re, c_im, d, w_glu):
    t = u.shape[0]
    chunk = S5_CHUNK
    cps = seq // chunk
    n_steps = max(1, (cps - 1).bit_length())
    tmat, bmat, cmat, scan = _s5_operators(a_re, a_im, log_dt, b_re, b_im, c_re, c_im, chunk, n_steps)
    nc = t // chunk
    u_g = (u.reshape(nc, chunk, S5_GROUPS, S5_GROUP).transpose(2, 0, 1, 3)
           .reshape(S5_GROUPS, nc, chunk * S5_GROUP).astype(BF16))
    y_g = _s5_scan(u_g, tmat, bmat, cmat, scan, cps)
    y = (y_g.reshape(S5_GROUPS, nc, chunk, S5_GROUP).transpose(1, 2, 0, 3).reshape(t, S5_WIDTH))
    return _s5_glu(y, u, d, w_glu)


def _softplus(v):
    return jnp.maximum(v, 0.0) + jnp.log1p(jnp.exp(-jnp.abs(v)))


def _ssd_kernel(zx_ref, dt_ref, dtt_ref, cw_ref, cb_ref, dtb_ref, dtbt_ref, alog_ref, alogt_ref,
                dskip_ref, nw_ref, o_ref, ext_ref, state_ref):
    L = SSD_CHUNK
    n = M2_STATE
    p = M2_HEADDIM

    @pl.when(pl.program_id(1) == 0)
    def _():
        ext_ref[0:SUBLANES, :] = jnp.zeros((SUBLANES, M2_CONV_DIM), F32)
        state_ref[...] = jnp.zeros_like(state_ref)

    z = zx_ref[:, :M2_INNER]
    xbc = zx_ref[:, M2_INNER:]
    ext_ref[SUBLANES:SUBLANES + L, :] = xbc
    conv = cb_ref[...] + cw_ref[M2_CONV - 1:M2_CONV, :] * xbc
    for k in range(M2_CONV - 1):
        lo = SUBLANES - (M2_CONV - 1) + k
        conv = conv + cw_ref[k:k + 1, :] * ext_ref[lo:lo + L, :]
    ext_ref[0:SUBLANES, :] = ext_ref[L:L + SUBLANES, :]
    xc = _silu(conv)
    xs = xc[:, :M2_INNER]
    bm = xc[:, M2_INNER:M2_INNER + M2_GROUPS * n]
    cm = xc[:, M2_INNER + M2_GROUPS * n:]

    dt = _softplus(dt_ref[...] + dtb_ref[...])
    a = -jnp.exp(alog_ref[...]) * dt
    a_t = -jnp.exp(alogt_ref[...]) * _softplus(dtt_ref[...] + dtbt_ref[...])
    ri = lax.broadcasted_iota(jnp.int32, (L, L), 0)
    ci = lax.broadcasted_iota(jnp.int32, (L, L), 1)
    causal = ci <= ri
    tril = causal.astype(F32)
    triu = (ri <= ci).astype(F32)
    acum_col = jnp.dot(tril, a, preferred_element_type=F32, precision=HIGHEST)
    acum_row = jnp.dot(a_t, triu, preferred_element_type=F32, precision=HIGHEST)

    ys = []
    rep = M2_HEADS // M2_GROUPS
    for g in range(M2_GROUPS):
        bg = bm[:, g * n:(g + 1) * n]
        cgb = cm[:, g * n:(g + 1) * n].astype(BF16)
        scores = lax.dot_general(cgb, bg.astype(BF16), (((1,), (1,)), ((), ())), preferred_element_type=F32)
        bg_t = bg.T.astype(BF16)
        for hh in range(rep):
            h = g * rep + hh
            ac = acum_col[:, h:h + 1]
            ar = acum_row[h:h + 1, :]
            decay = jnp.exp(jnp.where(causal, ac - ar, -jnp.inf))
            xh = xs[:, h * p:(h + 1) * p]
            xdt = xh * dt[:, h:h + 1]
            y = jnp.dot((scores * decay).astype(BF16), xdt.astype(BF16), preferred_element_type=F32)
            st = state_ref[h]
            y = y + jnp.dot(cgb, st.astype(BF16), preferred_element_type=F32) * jnp.exp(ac)
            y = y + dskip_ref[:, h * p:(h + 1) * p] * xh
            a_last = acum_col[L - 1:L, h:h + 1]
            xw = (xdt * jnp.exp(a_last - ac)).astype(BF16)
            state_ref[h] = jnp.exp(a_last) * st + jnp.dot(bg_t, xw, preferred_element_type=F32)
            ys.append(y)
    y = jnp.concatenate(ys, axis=1) * _silu(z)
    gw = M2_INNER // M2_GROUPS
    outs = []
    for g in range(M2_GROUPS):
        yg = y[:, g * gw:(g + 1) * gw]
        outs.append(yg * lax.rsqrt(jnp.mean(yg * yg, -1, keepdims=True) + RMS_EPS))
    o_ref[...] = (jnp.concatenate(outs, axis=1) * nw_ref[...]).astype(o_ref.dtype)


def _ssd_branch(zx, small, dt_col_block, batch, seq, conv_w, conv_b, dt_bias, a_log, d, norm_w):
    t = zx.shape[0]
    L = SSD_CHUNK
    ncs = seq // L
    hh = M2_HEADS
    dt_t = small[:, dt_col_block * LANES:dt_col_block * LANES + hh].T
    lane_pad = lambda v: jnp.zeros((1, LANES), F32).at[0, :hh].set(v.astype(F32))
    col = lambda v: v.astype(F32).reshape(hh, 1)
    dskip = jnp.repeat(d.astype(F32), M2_HEADDIM).reshape(1, M2_INNER)
    wz = M2_INNER + M2_CONV_DIM
    const = lambda shape: pl.BlockSpec(shape, lambda b, c: tuple(0 for _ in shape))
    return pl.pallas_call(
        _ssd_kernel,
        grid=(batch, ncs),
        in_specs=[pl.BlockSpec((L, wz), lambda b, c: (b * ncs + c, 0)),
                  pl.BlockSpec((L, LANES), lambda b, c: (b * ncs + c, dt_col_block)),
                  pl.BlockSpec((hh, L), lambda b, c: (0, b * ncs + c)),
                  const((M2_CONV, M2_CONV_DIM)), const((1, M2_CONV_DIM)),
                  const((1, LANES)), const((hh, 1)), const((1, LANES)), const((hh, 1)),
                  const((1, M2_INNER)), const((1, M2_INNER))],
        out_specs=pl.BlockSpec((L, M2_INNER), lambda b, c: (b * ncs + c, 0)),
        out_shape=jax.ShapeDtypeStruct((t, M2_INNER), BF16),
        scratch_shapes=[pltpu.VMEM((L + SUBLANES, M2_CONV_DIM), F32),
                        pltpu.VMEM((hh, M2_STATE, M2_HEADDIM), F32)],
        compiler_params=_cparams(("parallel", "arbitrary")),
        name="ssd",
    )(zx, small, dt_t, conv_w.astype(F32), conv_b.astype(F32).reshape(1, -1),
      lane_pad(dt_bias), col(dt_bias), lane_pad(a_log), col(a_log), dskip,
      norm_w.astype(F32).reshape(1, -1))


def _rot_cols(w):
    half = MLA_ROPE // 2
    return jnp.concatenate([-w[..., half:], w[..., :half]], axis=-1)


def _mla_prep_kernel(cq_ref, ckv_ref, krd_ref, ck_ref, sk_ref, qn_ref, kvn_ref, wq_ref, wkv_ref,
                     q_out, k_out, v_out):
    cq = cq_ref[...]
    qn = cq * lax.rsqrt(jnp.mean(cq * cq, -1, keepdims=True) + RMS_EPS) * qn_ref[...]
    q = jnp.dot(qn.astype(BF16), wq_ref[...], preferred_element_type=F32)
    ckv = ckv_ref[...]
    kvn = ckv * lax.rsqrt(jnp.mean(ckv * ckv, -1, keepdims=True) + RMS_EPS) * kvn_ref[...]
    kv = jnp.dot(kvn.astype(BF16), wkv_ref[...], preferred_element_type=F32)
    ck, sk = ck_ref[...], sk_ref[...]
    krd = krd_ref[...]
    shift = LANES - MLA_ROPE
    kr = krd * ck + pltpu.roll(krd, shift, axis=1) * sk
    scale = (MLA_NOPE + MLA_ROPE) ** -0.5
    lane = lax.broadcasted_iota(jnp.int32, (1, LANES), 1)
    cq_t = scale * (ck + (lane < MLA_NOPE).astype(F32))
    sq_t = scale * sk
    ones_col = (lane == MLA_V).astype(F32)
    for h in range(MLA_HEADS):
        qh = q[:, h * LANES:(h + 1) * LANES]
        q_out[h] = (qh * cq_t + pltpu.roll(qh, shift, axis=1) * sq_t).astype(BF16)
        k_out[h] = (kv[:, 2 * h * LANES:(2 * h + 1) * LANES] + kr).astype(BF16)
        v_out[h] = (kv[:, (2 * h + 1) * LANES:(2 * h + 2) * LANES] + ones_col).astype(BF16)


def _attn_kernel(q_ref, k_ref, v_ref, o_ref, m_ref, acc_ref):
    qi = pl.program_id(2)
    tq, tk = ATT_Q_BLOCK, ATT_KV_BLOCK
    per_q = tq // tk
    nt = (((1,), (1,)), ((), ()))
    m_ref[...] = jnp.full(m_ref.shape, -jnp.inf, F32)
    acc_ref[...] = jnp.zeros(acc_ref.shape, F32)

    def step(j, diag):
        start = pl.multiple_of(j * tk, tk)
        for hh in range(2):
            k = k_ref[hh, pl.ds(start, tk), :]
            v = v_ref[hh, pl.ds(start, tk), :]
            s = lax.dot_general(q_ref[hh], k, nt, preferred_element_type=F32)
            if diag is not None:
                ri = lax.broadcasted_iota(jnp.int32, (tq, tk), 0)
                ci = lax.broadcasted_iota(jnp.int32, (tq, tk), 1) + diag * tk
                s = jnp.where(ci <= ri, s, -jnp.inf)
            m_old = m_ref[hh]
            m_new = jnp.maximum(m_old, jnp.max(s, axis=1, keepdims=True))
            pexp = jnp.exp(s - m_new)
            acc_ref[hh] = acc_ref[hh] * jnp.exp(m_old - m_new) + jnp.dot(
                pexp.astype(BF16), v, preferred_element_type=F32)
            m_ref[hh] = m_new

    def body(j, carry):
        step(j, None)
        return carry

    lax.fori_loop(0, qi * per_q, body, 0)
    for dg in range(per_q):
        step(qi * per_q + dg, dg)
    outs = []
    for hh in range(2):
        acc = acc_ref[hh]
        outs.append(acc[:, :MLA_V] / acc[:, MLA_V:MLA_V + 1])
    o_ref[...] = jnp.concatenate(outs, axis=1).astype(o_ref.dtype)


def _mla_branch(small, positions, batch, seq, q_norm, w_uq, kv_norm, w_ukv):
    t = small.shape[0]
    hh = MLA_HEADS
    inv_freq = 1.0 / (ROPE_THETA ** (jnp.arange(0, MLA_ROPE, 2, dtype=F32) / MLA_ROPE))
    ang = positions.astype(F32).reshape(t, 1) * inv_freq
    cos2 = jnp.concatenate([jnp.cos(ang), jnp.cos(ang)], axis=-1)
    sin2 = jnp.concatenate([jnp.sin(ang), jnp.sin(ang)], axis=-1)
    ck = jnp.zeros((t, LANES), F32).at[:, MLA_NOPE:MLA_NOPE + MLA_ROPE].set(cos2)
    sk = jnp.zeros((t, LANES), F32).at[:, MLA_NOPE:MLA_NOPE + MLA_ROPE].set(sin2)
    wq = w_uq.astype(F32).reshape(MLA_Q_RANK, hh, MLA_NOPE + MLA_ROPE)
    wq = jnp.concatenate([wq, _rot_cols(wq[..., MLA_NOPE:])], axis=-1).reshape(MLA_Q_RANK, hh * LANES)
    wkv = w_ukv.astype(F32).reshape(MLA_KV_RANK, hh, MLA_NOPE + MLA_V)
    zpad = jnp.zeros((MLA_KV_RANK, hh, LANES - MLA_NOPE), F32)
    wkv = jnp.concatenate([wkv[..., :MLA_NOPE], zpad, wkv[..., MLA_NOPE:], zpad], axis=-1)
    wkv = wkv.reshape(MLA_KV_RANK, hh * 2 * LANES)
    tm = ROW_BLOCK
    const = lambda shape: pl.BlockSpec(shape, lambda i: tuple(0 for _ in shape))
    head_out = pl.BlockSpec((hh, tm, LANES), lambda i: (0, i, 0))
    q, k, v = pl.pallas_call(
        _mla_prep_kernel,
        grid=(t // tm,),
        in_specs=[pl.BlockSpec((tm, MLA_Q_RANK), lambda i: (i, 0)),
                  pl.BlockSpec((tm, MLA_KV_RANK), lambda i: (i, MLA_Q_RANK // MLA_KV_RANK)),
                  pl.BlockSpec((tm, LANES), lambda i: (i, (MLA_Q_RANK + MLA_KV_RANK) // LANES)),
                  pl.BlockSpec((tm, LANES), lambda i: (i, 0)),
                  pl.BlockSpec((tm, LANES), lambda i: (i, 0)),
                  const((1, MLA_Q_RANK)), const((1, MLA_KV_RANK)),
                  const((MLA_Q_RANK, hh * LANES)), const((MLA_KV_RANK, hh * 2 * LANES))],
        out_specs=[head_out, head_out, head_out],
        out_shape=[jax.ShapeDtypeStruct((hh, t, LANES), BF16)] * 3,
        compiler_params=_cparams(("parallel",)),
        name="mla_prep",
    )(small, small, small, ck, sk, q_norm.astype(F32).reshape(1, -1), kv_norm.astype(F32).reshape(1, -1),
      wq.astype(BF16), wkv.astype(BF16))
    blk = ATT_Q_BLOCK
    nq = seq // blk
    return pl.pallas_call(
        _attn_kernel,
        grid=(batch, hh // 2, nq),
        in_specs=[pl.BlockSpec((2, blk, LANES), lambda b, hp, i: (hp, b * nq + i, 0)),
                  pl.BlockSpec((2, seq, LANES), lambda b, hp, i: (hp, b, 0)),
                  pl.BlockSpec((2, seq, LANES), lambda b, hp, i: (hp, b, 0))],
        out_specs=pl.BlockSpec((blk, LANES), lambda b, hp, i: (b * nq + i, hp)),
        out_shape=jax.ShapeDtypeStruct((t, MLA_WIDTH), BF16),
        scratch_shapes=[pltpu.VMEM((2, blk, 1), F32), pltpu.VMEM((2, blk, LANES), F32)],
        compiler_params=_cparams(("parallel", "parallel", "arbitrary")),
        name="mla_attn",
    )(q, k, v)


def _merge_kernel(h_ref, ya_ref, yb_ref, yc_ref, g_ref, wa_ref, wb_ref, wc_ref, wo_ref, lg_ref, lb_ref,
                  wq_ref, h1_ref, qp_ref):
    d = D_MODEL
    gates = _sigmoid(g_ref[...].astype(F32))
    merged = (gates[:, :d] * jnp.dot(ya_ref[...], wa_ref[...], preferred_element_type=F32)
              + gates[:, d:2 * d] * jnp.dot(yb_ref[...], wb_ref[...], preferred_element_type=F32)
              + gates[:, 2 * d:] * jnp.dot(yc_ref[...], wc_ref[...], preferred_element_type=F32))
    mix = jnp.dot(merged.astype(BF16), wo_ref[...], preferred_element_type=F32)
    h1 = _ln_rows(DN_ALPHA * h_ref[...] + mix, lg_ref[...], lb_ref[...])
    h1_ref[...] = h1
    qp_ref[...] = jnp.dot(h1.astype(BF16), wq_ref[...], preferred_element_type=F32).astype(qp_ref.dtype)


def _merge(h, ya, yb, yc, gates, wa, wb, wc, wo, lg, lb, wq):
    t, d = h.shape
    tm = ROW_BLOCK
    row = lambda w: pl.BlockSpec((tm, w), lambda i: (i, 0))
    const = lambda a: pl.BlockSpec(a.shape, lambda i: (0, 0))
    ws = [wa.astype(BF16), wb.astype(BF16), wc.astype(BF16), wo.astype(BF16),
          lg.astype(F32).reshape(1, d), lb.astype(F32).reshape(1, d), wq.astype(BF16)]
    nq = wq.shape[1]
    return pl.pallas_call(
        _merge_kernel,
        grid=(t // tm,),
        in_specs=[row(d), row(ya.shape[1]), row(yb.shape[1]), row(yc.shape[1]), row(gates.shape[1])]
                 + [const(a) for a in ws],
        out_specs=[row(d), row(nq)],
        out_shape=[jax.ShapeDtypeStruct((t, d), F32), jax.ShapeDtypeStruct((t, nq), BF16)],
        compiler_params=_cparams(("parallel",)),
        name="merge",
    )(h, ya, yb, yc, gates, *ws)


_PEER_PAIRS = tuple((a, b) for a in range(PEER_TOPK) for b in range(PEER_TOPK) if (a + 1) * (b + 1) <= PEER_TOPK)


def _route_kernel(qp_ref, sk_ref, off_ref, par_ref, gate_ref, s_scr, i_scr):
    tt = PEER_ROUTE_BLOCK
    nk = PEER_NKEYS
    kd = PEER_KEY_DIM // 2
    nt = (((1,), (1,)), ((), ()))
    key_iota = lax.broadcasted_iota(jnp.int32, (nk, tt), 0)

    def head(h, carry):
        vals = []
        for j in range(2):
            q = qp_ref[:, pl.ds(pl.multiple_of((2 * h + j) * kd, kd), kd)]
            vals.append(lax.dot_general(sk_ref[j], q, nt, preferred_element_type=F32))
        for r in range(PEER_TOPK):
            for j in range(2):
                m = jnp.max(vals[j], axis=0, keepdims=True)
                i = jnp.min(jnp.where(vals[j] == m, key_iota, nk), axis=0, keepdims=True)
                vals[j] = jnp.where(key_iota == i, -jnp.inf, vals[j])
                s_scr[j * PEER_TOPK + r, pl.ds(h, 1), :] = m
                i_scr[j * PEER_TOPK + r, pl.ds(h, 1), :] = i
        return carry

    lax.fori_loop(0, PEER_HEADS, head, 0)

    s1 = [s_scr[r] for r in range(PEER_TOPK)]
    s2 = [s_scr[PEER_TOPK + r] for r in range(PEER_TOPK)]
    i1 = [i_scr[r] for r in range(PEER_TOPK)]
    i2 = [i_scr[PEER_TOPK + r] for r in range(PEER_TOPK)]
    cand = [s1[a] + s2[b] for a, b in _PEER_PAIRS]
    cidx = [i1[a] * nk + i2[b] for a, b in _PEER_PAIRS]
    best_s, best_i = [], []
    for r in range(PEER_TOPK):
        m = functools.reduce(jnp.maximum, cand)
        found = jnp.zeros(m.shape, jnp.bool_)
        sel = jnp.zeros(m.shape, jnp.int32)
        for c in range(len(cand)):
            hit = cand[c] == m
            take = jnp.logical_and(hit, jnp.logical_not(found))
            found = jnp.logical_or(found, hit)
            sel = jnp.where(take, cidx[c], sel)
            cand[c] = jnp.where(take, -jnp.inf, cand[c])
        best_s.append(m)
        best_i.append(sel)
    e = [jnp.exp(s - best_s[0]) for s in best_s]
    tot = functools.reduce(lambda p, q: p + q, e)
    sel_all = jnp.concatenate(best_i, axis=0)
    off_ref[...] = (sel_all >> 1).T
    par_ref[...] = (sel_all & 1).astype(F32).T
    gate_ref[...] = (jnp.concatenate(e, axis=0) / jnp.concatenate([tot] * PEER_TOPK, axis=0)).T


def _peer_route(qp, subkeys):
    t = qp.shape[0]
    tt = PEER_ROUTE_BLOCK
    out = pl.BlockSpec((tt, PEER_TOPK * PEER_HEADS), lambda i: (i, 0))
    shp = (t, PEER_TOPK * PEER_HEADS)
    off, par, gate = pl.pallas_call(
        _route_kernel,
        grid=(t // tt,),
        in_specs=[pl.BlockSpec((tt, qp.shape[1]), lambda i: (i, 0)),
                  pl.BlockSpec(subkeys.shape, lambda i: (0, 0, 0))],
        out_specs=[out, out, out],
        out_shape=[jax.ShapeDtypeStruct(shp, jnp.int32), jax.ShapeDtypeStruct(shp, F32),
                   jax.ShapeDtypeStruct(shp, F32)],
        scratch_shapes=[pltpu.VMEM((2 * PEER_TOPK, PEER_HEADS, tt), F32),
                        pltpu.VMEM((2 * PEER_TOPK, PEER_HEADS, tt), jnp.int32)],
        compiler_params=_cparams(("parallel",)),
        name="peer_route",
    )(qp, subkeys.astype(BF16))
    return off.reshape(-1), par, gate


NSEL = PEER_HEADS * PEER_TOPK
WROWS = NSEL * BF16_ROWS


def _peer_tables(tab):
    e, d = tab.shape
    c = d // LANES
    return tab.astype(BF16).reshape(e // 2, 2, c, LANES).transpose(0, 2, 1, 3).reshape(e // 2, 2 * c, LANES)


def _expand_consts():
    k = np.arange(WROWS) // BF16_ROWS
    r = np.arange(WROWS) % BF16_ROWS
    e = (k[None, :] == np.arange(NSEL)[:, None]).astype(np.float32)
    e2 = np.concatenate([e * (r % 2 == 0)[None, :], e * (r % 2 == 1)[None, :]], axis=0)
    return e, e2


def _gather_tiles(off_ref, tab_ref, w_scr, t):
    base = t * NSEL
    for k in range(NSEL):
        w_scr[k * BF16_ROWS:(k + 1) * BF16_ROWS, :] = tab_ref[off_ref[base + k]]


def _peer_up_kernel(off_ref, x_ref, par_ref, gate_ref, tab_ref, e_ref, et_ref, act_ref, w_scr, pe_scr, zz_scr):
    tt = PEER_BLOCK
    nt = (((1,), (1,)), ((), ()))
    par = par_ref[...]
    pe_scr[...] = jnp.dot(par.astype(BF16), e_ref[...], preferred_element_type=F32)
    row_r = lax.broadcasted_iota(jnp.int32, (SUBLANES, WROWS), 1) % BF16_ROWS
    chunk = lax.broadcasted_iota(jnp.int32, (SUBLANES, WROWS), 0)
    want_even = (row_r - 2 * chunk).astype(F32)

    def tok(t, carry):
        _gather_tiles(off_ref, tab_ref, w_scr, t)
        r = lax.dot_general(x_ref[t].astype(BF16), w_scr[...], nt, preferred_element_type=F32)
        mask = want_even == pe_scr[pl.ds(t, 1), :]
        zz_scr[pl.ds(pl.multiple_of(t * SUBLANES, SUBLANES), SUBLANES), :] = jnp.where(mask, r, 0.0)
        return carry

    lax.fori_loop(0, tt, tok, 0, unroll=16)
    hd8 = jnp.dot(zz_scr[...].astype(BF16), et_ref[...], preferred_element_type=F32)
    hidden = jnp.sum(hd8.reshape(tt, SUBLANES, NSEL), axis=1)
    act = gate_ref[...] * _gelu_tanh(hidden)
    act_ref[:, :NSEL] = act * (1.0 - par)
    act_ref[:, NSEL:] = act * par


def _peer_down_kernel(off_ref, act_ref, h_ref, tab_ref, e2_ref, lg_ref, lb_ref, o_ref, w_scr, ae_scr, f_scr):
    tt = PEER_BLOCK
    ae_scr[...] = jnp.dot(act_ref[...].astype(BF16), e2_ref[...], preferred_element_type=F32)
    row_r = lax.broadcasted_iota(jnp.int32, (SUBLANES, WROWS), 1) % BF16_ROWS
    chunk = lax.broadcasted_iota(jnp.int32, (SUBLANES, WROWS), 0)
    m_chunk = (row_r >> 1) == chunk

    def tok(t, carry):
        _gather_tiles(off_ref, tab_ref, w_scr, t)
        lhs = jnp.where(m_chunk, ae_scr[pl.ds(t, 1), :], 0.0).astype(BF16)
        f_scr[t] = jnp.dot(lhs, w_scr[...], preferred_element_type=F32)
        return carry

    lax.fori_loop(0, tt, tok, 0, unroll=16)
    v = DN_ALPHA * h_ref[...] + f_scr[...]
    inv_d = 1.0 / D_MODEL
    mu = jnp.sum(jnp.sum(v, axis=2, keepdims=True), axis=1, keepdims=True) * inv_d
    vc = v - mu
    var = jnp.sum(jnp.sum(vc * vc, axis=2, keepdims=True), axis=1, keepdims=True) * inv_d
    o_ref[...] = vc * lax.rsqrt(var + LN_EPS) * lg_ref[...] + lb_ref[...]


def _peer(h1, qp, subkeys, u_tab, v_tab, lg, lb):
    t, d = h1.shape
    tt = PEER_BLOCK
    c = d // LANES
    off, par, gate = _peer_route(qp, subkeys)
    e, e2 = _expand_consts()
    e_bf, et_bf, e2_bf = jnp.asarray(e, BF16), jnp.asarray(e.T, BF16), jnp.asarray(e2, BF16)
    utab, vtab = _peer_tables(u_tab), _peer_tables(v_tab)
    h3 = h1.reshape(t, c, LANES)
    smem = pl.BlockSpec((tt * NSEL,), lambda i: (i,), memory_space=pltpu.SMEM)
    resident = lambda a: pl.BlockSpec(a.shape, lambda i: tuple(0 for _ in a.shape), pipeline_mode=pl.Buffered(1))
    rows = lambda w: pl.BlockSpec((tt, w), lambda i: (i, 0))
    tiles = pl.BlockSpec((tt, c, LANES), lambda i: (i, 0, 0))
    act = pl.pallas_call(
        _peer_up_kernel,
        grid=(t // tt,),
        in_specs=[smem, tiles, rows(NSEL), rows(NSEL), resident(utab), resident(e_bf), resident(et_bf)],
        out_specs=rows(2 * NSEL),
        out_shape=jax.ShapeDtypeStruct((t, 2 * NSEL), F32),
        scratch_shapes=[pltpu.VMEM((WROWS, LANES), BF16), pltpu.VMEM((tt, WROWS), F32),
                        pltpu.VMEM((tt * SUBLANES, WROWS), F32)],
        compiler_params=_cparams(("parallel",)),
        name="peer_up",
    )(off, h3, par, gate, utab, e_bf, et_bf)
    lg3 = lg.astype(F32).reshape(1, c, LANES)
    lb3 = lb.astype(F32).reshape(1, c, LANES)
    out = pl.pallas_call(
        _peer_down_kernel,
        grid=(t // tt,),
        in_specs=[smem, rows(2 * NSEL), tiles, resident(vtab), resident(e2_bf), resident(lg3), resident(lb3)],
        out_specs=tiles,
        out_shape=jax.ShapeDtypeStruct((t, c, LANES), F32),
        scratch_shapes=[pltpu.VMEM((WROWS, LANES), BF16), pltpu.VMEM((tt, WROWS), F32),
                        pltpu.VMEM((tt, c, LANES), F32)],
        compiler_params=_cparams(("parallel",)),
        name="peer_down",
    )(off, act, h3, vtab, e2_bf, lg3, lb3)
    return out.reshape(t, d)


def kernel(x, positions, ln_in_g, ln_in_b, w_in, s5_a_re, s5_a_im, s5_log_dt, s5_b_re, s5_b_im, s5_c_re, s5_c_im, s5_d, s5_w_glu, m2_conv_w, m2_conv_b, m2_dt_bias, m2_a_log, m2_d, m2_norm_w, mla_q_norm, mla_w_uq, mla_kv_norm, mla_w_ukv, w_br_a, w_br_b, w_br_c, w_out, ln1_g, ln1_b, peer_w_q, peer_subkeys, peer_u, peer_v, ln2_g, ln2_b):
    b, s, d = x.shape
    t = b * s
    assert d == D_MODEL and s % ATT_Q_BLOCK == 0 and ATT_Q_BLOCK % ATT_KV_BLOCK == 0 and s % SSD_CHUNK == 0 and t % ROW_BLOCK == 0
    c_s5, c_z, c_xbc, c_dt, c_q, c_kv, c_kr, _ = np.cumsum(IN_SPLITS).tolist()
    h = _layer_norm_in(x.reshape(t, d).astype(F32), ln_in_g.astype(F32), ln_in_b.astype(F32))
    for l in range(DEPTH):
        w = w_in[l].astype(F32)
        w_kr = w[:, c_kv:c_kr]
        w_small = jnp.concatenate(
            [w[:, c_dt:c_kv], w[:, c_xbc:c_dt], jnp.zeros((d, LANES // 2 - M2_HEADS), F32), w_kr, _rot_cols(w_kr)],
            axis=1)
        u = _matmul(h, w[:, :c_s5].astype(BF16), F32, S5_WIDTH, "proj_s5")
        zx = _matmul(h, w[:, c_s5:c_xbc].astype(BF16), F32, (c_xbc - c_s5) // 2, "proj_ssd")
        small = _matmul(h, w_small.astype(BF16), F32, w_small.shape[1], "proj_small")
        gates = _matmul(h, w[:, c_kr:].astype(BF16), BF16, D_MODEL, "proj_gates")
        ya = _s5_branch(u, s, s5_a_re[l], s5_a_im[l], s5_log_dt[l], s5_b_re[l], s5_b_im[l],
                        s5_c_re[l], s5_c_im[l], s5_d[l], s5_w_glu[l])
        yb = _ssd_branch(zx, small, (MLA_Q_RANK + MLA_KV_RANK) // LANES, b, s, m2_conv_w[l], m2_conv_b[l],
                         m2_dt_bias[l], m2_a_log[l], m2_d[l], m2_norm_w[l])
        yc = _mla_branch(small, positions, b, s, mla_q_norm[l], mla_w_uq[l], mla_kv_norm[l], mla_w_ukv[l])
        h1, qp = _merge(h, ya, yb, yc, gates, w_br_a[l], w_br_b[l], w_br_c[l], w_out[l], ln1_g[l], ln1_b[l],
                        peer_w_q[l])
        h = _peer(h1, qp, peer_subkeys[l], peer_u[l], peer_v[l], ln2_g[l], ln2_b[l])
    return h.reshape(b, s, d).astype(x.dtype)
```

```python
import functools
import math

import numpy as np
import jax
import jax.numpy as jnp
from jax import lax
from jax.experimental import pallas as pl
from jax.experimental.pallas import tpu as pltpu

F32 = jnp.float32
BF16 = jnp.bfloat16
HIGHEST = lax.Precision.HIGHEST

D_MODEL = 1024
DEPTH = 2
S5_GROUP = 16
S5_GROUPS = 24
S5_WIDTH = S5_GROUP * S5_GROUPS
S5_STATE = 64
M2_HEADDIM = 64
M2_INNER = 512
M2_HEADS = M2_INNER // M2_HEADDIM
M2_GROUPS = 2
M2_STATE = 128
M2_CONV = 4
M2_CONV_DIM = M2_INNER + 2 * M2_GROUPS * M2_STATE
MLA_HEADS = 8
MLA_Q_RANK = 256
MLA_KV_RANK = 128
MLA_NOPE = 64
MLA_ROPE = 32
MLA_V = 64
MLA_WIDTH = MLA_HEADS * MLA_V
ROPE_THETA = 10000.0
PEER_HEADS = 8
PEER_NKEYS = 128
PEER_EXPERTS = PEER_NKEYS * PEER_NKEYS
PEER_KEY_DIM = 256
PEER_TOPK = 16
N_BRANCHES = 3
IN_SPLITS = (S5_WIDTH, M2_INNER, M2_CONV_DIM, M2_HEADS, MLA_Q_RANK, MLA_KV_RANK, MLA_ROPE, N_BRANCHES * D_MODEL)
DN_ALPHA = (2 * DEPTH) ** 0.25
LN_EPS = 1e-5
RMS_EPS = 1e-6

LANES = 128
SUBLANES = 8
BF16_ROWS = 16
VMEM_LIMIT = 56 * 1024 * 1024

S5_CHUNK = 32
SSD_CHUNK = 128
ATT_Q_BLOCK = 512
ATT_KV_BLOCK = 512
ROW_BLOCK = 512
PEER_ROUTE_BLOCK = 256
PEER_BLOCK = 128


def _cparams(sem):
    return pltpu.CompilerParams(dimension_semantics=sem, vmem_limit_bytes=VMEM_LIMIT)


def _ln_rows(v, g, b):
    mu = jnp.mean(v, -1, keepdims=True)
    vc = v - mu
    var = jnp.mean(vc * vc, -1, keepdims=True)
    return vc * lax.rsqrt(var + LN_EPS) * g + b


def _gelu_tanh(v):
    return 0.5 * v * (1.0 + jnp.tanh(math.sqrt(2.0 / math.pi) * (v + 0.044715 * (v * v * v))))


def _sigmoid(v):
    return 1.0 / (1.0 + jnp.exp(-v))


def _silu(v):
    return v * _sigmoid(v)


def _ln_kernel(x_ref, g_ref, b_ref, o_ref):
    o_ref[...] = _ln_rows(x_ref[...], g_ref[...], b_ref[...])


def _layer_norm_in(xt, g, b):
    t, d = xt.shape
    return pl.pallas_call(
        _ln_kernel,
        grid=(t // ROW_BLOCK,),
        in_specs=[pl.BlockSpec((ROW_BLOCK, d), lambda i: (i, 0)),
                  pl.BlockSpec((1, d), lambda i: (0, 0)),
                  pl.BlockSpec((1, d), lambda i: (0, 0))],
        out_specs=pl.BlockSpec((ROW_BLOCK, d), lambda i: (i, 0)),
        out_shape=jax.ShapeDtypeStruct((t, d), F32),
        compiler_params=_cparams(("parallel",)),
        name="ln_in",
    )(xt, g.reshape(1, d), b.reshape(1, d))


def _mm_kernel(x_ref, w_ref, o_ref):
    o_ref[...] = jnp.dot(x_ref[...].astype(BF16), w_ref[...], preferred_element_type=F32).astype(o_ref.dtype)


def _matmul(x, w, out_dtype, tn, name):
    t, k = x.shape
    n = w.shape[1]
    assert n % tn == 0 and t % ROW_BLOCK == 0
    return pl.pallas_call(
        _mm_kernel,
        grid=(t // ROW_BLOCK, n // tn),
        in_specs=[pl.BlockSpec((ROW_BLOCK, k), lambda i, j: (i, 0)),
                  pl.BlockSpec((k, tn), lambda i, j: (0, j))],
        out_specs=pl.BlockSpec((ROW_BLOCK, tn), lambda i, j: (i, j)),
        out_shape=jax.ShapeDtypeStruct((t, n), out_dtype),
        compiler_params=_cparams(("parallel", "arbitrary")),
        name=name,
    )(x, w)


def _s5_operators(a_re, a_im, log_dt, b_re, b_im, c_re, c_im, chunk, n_steps):
    lr, li = a_re.astype(F32), a_im.astype(F32)
    dt = jnp.exp(log_dt.astype(F32))[:, None]

    def apow(n):
        n = jnp.asarray(n, F32)[..., None, None]
        mag = jnp.exp(lr * dt * n)
        return mag * jnp.cos(li * dt * n), mag * jnp.sin(li * dt * n)

    abar_re, abar_im = apow(1.0)
    nr, ni = abar_re - 1.0, abar_im
    den = lr * lr + li * li
    coef_re = ((nr * lr + ni * li) / den)[..., None]
    coef_im = ((ni * lr - nr * li) / den)[..., None]
    br, bi = b_re.astype(F32), b_im.astype(F32)
    bbar_re = coef_re * br - coef_im * bi
    bbar_im = coef_re * bi + coef_im * br
    cr, ci = c_re.astype(F32), c_im.astype(F32)

    lags = jnp.arange(chunk + 1)
    pr, pi = apow(lags)
    cp_re = cr[None] * pr[:, :, None, :] - ci[None] * pi[:, :, None, :]
    cp_im = cr[None] * pi[:, :, None, :] + ci[None] * pr[:, :, None, :]
    g = lr.shape[0]
    w = chunk * S5_GROUP
    kern = (jnp.einsum('jghp,gpk->gkjh', cp_re[:chunk], bbar_re, precision=HIGHEST)
            - jnp.einsum('jghp,gpk->gkjh', cp_im[:chunk], bbar_im, precision=HIGHEST))
    kern = kern.reshape(g, S5_GROUP, w)
    tt = jnp.arange(chunk)
    tmat = jnp.stack([jnp.pad(kern[:, :, :w - s * S5_GROUP], ((0, 0), (0, 0), (s * S5_GROUP, 0)))
                      for s in range(chunk)], axis=1).reshape(g, w, w)

    rev = chunk - 1 - tt
    bm_re = pr[rev][:, :, :, None] * bbar_re[None] - pi[rev][:, :, :, None] * bbar_im[None]
    bm_im = pr[rev][:, :, :, None] * bbar_im[None] + pi[rev][:, :, :, None] * bbar_re[None]
    bmat = jnp.concatenate([bm_re, bm_im], axis=2)
    bmat = bmat.transpose(1, 0, 3, 2).reshape(g, w, 2 * S5_STATE)

    cm = jnp.concatenate([cp_re[1:], -cp_im[1:]], axis=3)
    cmat = cm.transpose(1, 3, 0, 2).reshape(g, 2 * S5_STATE, w)

    steps = chunk * (2 ** jnp.arange(n_steps))
    sr, si = apow(steps)
    par = jnp.concatenate([sr, sr], axis=-1)
    pai = jnp.concatenate([-si, si], axis=-1)
    scan = jnp.stack([par, pai], axis=2).transpose(1, 0, 2, 3)
    return tmat.astype(BF16), bmat.astype(BF16), cmat.astype(BF16), scan


def _s5_kernel(u_ref, t_ref, b_ref, c_ref, scan_ref, y_ref, *, chunks_per_seq, n_steps):
    u = u_ref[0]
    nc = u.shape[0]
    z = jnp.dot(u, b_ref[0], preferred_element_type=F32)
    row = lax.broadcasted_iota(jnp.int32, (nc, 1), 0) % chunks_per_seq
    for k in range(n_steps):
        sh = 1 << k
        zs = jnp.where(row >= sh, pltpu.roll(z, sh, axis=0), 0.0)
        zsw = pltpu.roll(zs, S5_STATE, axis=1)
        z = z + scan_ref[0, k, 0:1, :] * zs + scan_ref[0, k, 1:2, :] * zsw
    s0 = jnp.where(row >= 1, pltpu.roll(z, 1, axis=0), 0.0)
    y = jnp.dot(u, t_ref[0], preferred_element_type=F32)
    y = y + jnp.dot(s0.astype(BF16), c_ref[0], preferred_element_type=F32)
    y_ref[0] = y


def _s5_scan(u_g, tmat, bmat, cmat, scan, chunks_per_seq):
    g, nc, w = u_g.shape
    n_steps = scan.shape[1]
    return pl.pallas_call(
        functools.partial(_s5_kernel, chunks_per_seq=chunks_per_seq, n_steps=n_steps),
        grid=(g,),
        in_specs=[pl.BlockSpec((1, nc, w), lambda i: (i, 0, 0)),
                  pl.BlockSpec((1, w, w), lambda i: (i, 0, 0)),
                  pl.BlockSpec((1, w, 2 * S5_STATE), lambda i: (i, 0, 0)),
                  pl.BlockSpec((1, 2 * S5_STATE, w), lambda i: (i, 0, 0)),
                  pl.BlockSpec((1, n_steps, 2, 2 * S5_STATE), lambda i: (i, 0, 0, 0))],
        out_specs=pl.BlockSpec((1, nc, w), lambda i: (i, 0, 0)),
        out_shape=jax.ShapeDtypeStruct((g, nc, w), F32),
        compiler_params=_cparams(("parallel",)),
        name="s5_scan",
    )(u_g, tmat, bmat, cmat, scan)


def _s5_glu_kernel(y_ref, u_ref, d_ref, w_ref, o_ref):
    v = y_ref[...] + d_ref[...] * u_ref[...]
    z = _gelu_tanh(v)
    gate = _sigmoid(jnp.dot(z.astype(BF16), w_ref[...], preferred_element_type=F32))
    o_ref[...] = (z * gate).astype(o_ref.dtype)


def _s5_glu(y, u, d, w_glu):
    t, w = y.shape
    row = pl.BlockSpec((ROW_BLOCK, w), lambda i: (i, 0))
    return pl.pallas_call(
        _s5_glu_kernel,
        grid=(t // ROW_BLOCK,),
        in_specs=[row, row, pl.BlockSpec((1, w), lambda i: (0, 0)), pl.BlockSpec((w, w), lambda i: (0, 0))],
        out_specs=row,
        out_shape=jax.ShapeDtypeStruct((t, w), BF16),
        compiler_params=_cparams(("parallel",)),
        name="s5_glu",
    )(y, u, d.reshape(1, w).astype(F32), w_glu.astype(BF16))


def _s5_branch(u, seq, a_re, a_im, log_dt, b_re, b_im, c_re, c_im, d, w_glu):
    t = u.shape[0]
    chunk = S5_CHUNK
    cps = seq // chunk
    n_steps = max(1, (cps - 1).bit_length())
    tmat, bmat, cmat, scan = _s5_operators(a_re, a_im, log_dt, b_re, b_im, c_re, c_im, chunk, n_steps)
    nc = t // chunk
    u_g = (u.reshape(nc, chunk, S5_GROUPS, S5_GROUP).transpose(2, 0, 1, 3)
           .reshape(S5_GROUPS, nc, chunk * S5_GROUP).astype(BF16))
    y_g = _s5_scan(u_g, tmat, bmat, cmat, scan, cps)
    y = (y_g.reshape(S5_GROUPS, nc, chunk, S5_GROUP).transpose(1, 2, 0, 3).reshape(t, S5_WIDTH))
    return _s5_glu(y, u, d, w_glu)


def _softplus(v):
    return jnp.maximum(v, 0.0) + jnp.log1p(jnp.exp(-jnp.abs(v)))


def _ssd_kernel(zx_ref, dt_ref, dtt_ref, cw_ref, cb_ref, dtb_ref, dtbt_ref, alog_ref, alogt_ref,
                dskip_ref, nw_ref, o_ref, ext_ref, state_ref):
    L = SSD_CHUNK
    n = M2_STATE
    p = M2_HEADDIM

    @pl.when(pl.program_id(1) == 0)
    def _():
        ext_ref[0:SUBLANES, :] = jnp.zeros((SUBLANES, M2_CONV_DIM), F32)
        state_ref[...] = jnp.zeros_like(state_ref)

    z = zx_ref[:, :M2_INNER]
    xbc = zx_ref[:, M2_INNER:]
    ext_ref[SUBLANES:SUBLANES + L, :] = xbc
    conv = cb_ref[...] + cw_ref[M2_CONV - 1:M2_CONV, :] * xbc
    for k in range(M2_CONV - 1):
        lo = SUBLANES - (M2_CONV - 1) + k
        conv = conv + cw_ref[k:k + 1, :] * ext_ref[lo:lo + L, :]
    ext_ref[0:SUBLANES, :] = ext_ref[L:L + SUBLANES, :]
    xc = _silu(conv)
    xs = xc[:, :M2_INNER]
    bm = xc[:, M2_INNER:M2_INNER + M2_GROUPS * n]
    cm = xc[:, M2_INNER + M2_GROUPS * n:]

    dt = _softplus(dt_ref[...] + dtb_ref[...])
    a = -jnp.exp(alog_ref[...]) * dt
    a_t = -jnp.exp(alogt_ref[...]) * _softplus(dtt_ref[...] + dtbt_ref[...])
    ri = lax.broadcasted_iota(jnp.int32, (L, L), 0)
    ci = lax.broadcasted_iota(jnp.int32, (L, L), 1)
    causal = ci <= ri
    tril = causal.astype(F32)
    triu = (ri <= ci).astype(F32)
    acum_col = jnp.dot(tril, a, preferred_element_type=F32, precision=HIGHEST)
    acum_row = jnp.dot(a_t, triu, preferred_element_type=F32, precision=HIGHEST)

    ys = []
    rep = M2_HEADS // M2_GROUPS
    for g in range(M2_GROUPS):
        bg = bm[:, g * n:(g + 1) * n]
        cgb = cm[:, g * n:(g + 1) * n].astype(BF16)
        scores = lax.dot_general(cgb, bg.astype(BF16), (((1,), (1,)), ((), ())), preferred_element_type=F32)
        bg_t = bg.T.astype(BF16)
        for hh in range(rep):
            h = g * rep + hh
            ac = acum_col[:, h:h + 1]
            ar = acum_row[h:h + 1, :]
            decay = jnp.exp(jnp.where(causal, ac - ar, -jnp.inf))
            xh = xs[:, h * p:(h + 1) * p]
            xdt = xh * dt[:, h:h + 1]
            y = jnp.dot((scores * decay).astype(BF16), xdt.astype(BF16), preferred_element_type=F32)
            st = state_ref[h]
            y = y + jnp.dot(cgb, st.astype(BF16), preferred_element_type=F32) * jnp.exp(ac)
            y = y + dskip_ref[:, h * p:(h + 1) * p] * xh
            a_last = acum_col[L - 1:L, h:h + 1]
            xw = (xdt * jnp.exp(a_last - ac)).astype(BF16)
            state_ref[h] = jnp.exp(a_last) * st + jnp.dot(bg_t, xw, preferred_element_type=F32)
            ys.append(y)
    y = jnp.concatenate(ys, axis=1) * _silu(z)
    gw = M2_INNER // M2_GROUPS
    outs = []
    for g in range(M2_GROUPS):
        yg = y[:, g * gw:(g + 1) * gw]
        outs.append(yg * lax.rsqrt(jnp.mean(yg * yg, -1, keepdims=True) + RMS_EPS))
    o_ref[...] = (jnp.concatenate(outs, axis=1) * nw_ref[...]).astype(o_ref.dtype)


def _ssd_branch(zx, small, dt_col_block, batch, seq, conv_w, conv_b, dt_bias, a_log, d, norm_w):
    t = zx.shape[0]
    L = SSD_CHUNK
    ncs = seq // L
    hh = M2_HEADS
    dt_t = small[:, dt_col_block * LANES:dt_col_block * LANES + hh].T
    lane_pad = lambda v: jnp.zeros((1, LANES), F32).at[0, :hh].set(v.astype(F32))
    col = lambda v: v.astype(F32).reshape(hh, 1)
    dskip = jnp.repeat(d.astype(F32), M2_HEADDIM).reshape(1, M2_INNER)
    wz = M2_INNER + M2_CONV_DIM
    const = lambda shape: pl.BlockSpec(shape, lambda b, c: tuple(0 for _ in shape))
    return pl.pallas_call(
        _ssd_kernel,
        grid=(batch, ncs),
        in_specs=[pl.BlockSpec((L, wz), lambda b, c: (b * ncs + c, 0)),
                  pl.BlockSpec((L, LANES), lambda b, c: (b * ncs + c, dt_col_block)),
                  pl.BlockSpec((hh, L), lambda b, c: (0, b * ncs + c)),
                  const((M2_CONV, M2_CONV_DIM)), const((1, M2_CONV_DIM)),
                  const((1, LANES)), const((hh, 1)), const((1, LANES)), const((hh, 1)),
                  const((1, M2_INNER)), const((1, M2_INNER))],
        out_specs=pl.BlockSpec((L, M2_INNER), lambda b, c: (b * ncs + c, 0)),
        out_shape=jax.ShapeDtypeStruct((t, M2_INNER), BF16),
        scratch_shapes=[pltpu.VMEM((L + SUBLANES, M2_CONV_DIM), F32),
                        pltpu.VMEM((hh, M2_STATE, M2_HEADDIM), F32)],
        compiler_params=_cparams(("parallel", "arbitrary")),
        name="ssd",
    )(zx, small, dt_t, conv_w.astype(F32), conv_b.astype(F32).reshape(1, -1),
      lane_pad(dt_bias), col(dt_bias), lane_pad(a_log), col(a_log), dskip,
      norm_w.astype(F32).reshape(1, -1))


def _rot_cols(w):
    half = MLA_ROPE // 2
    return jnp.concatenate([-w[..., half:], w[..., :half]], axis=-1)


def _mla_prep_kernel(cq_ref, ckv_ref, krd_ref, ck_ref, sk_ref, qn_ref, kvn_ref, wq_ref, wkv_ref,
                     q_out, k_out, v_out):
    cq = cq_ref[...]
    qn = cq * lax.rsqrt(jnp.mean(cq * cq, -1, keepdims=True) + RMS_EPS) * qn_ref[...]
    q = jnp.dot(qn.astype(BF16), wq_ref[...], preferred_element_type=F32)
    ckv = ckv_ref[...]
    kvn = ckv * lax.rsqrt(jnp.mean(ckv * ckv, -1, keepdims=True) + RMS_EPS) * kvn_ref[...]
    kv = jnp.dot(kvn.astype(BF16), wkv_ref[...], preferred_element_type=F32)
    ck, sk = ck_ref[...], sk_ref[...]
    krd = krd_ref[...]
    shift = LANES - MLA_ROPE
    kr = krd * ck + pltpu.roll(krd, shift, axis=1) * sk
    scale = (MLA_NOPE + MLA_ROPE) ** -0.5
    lane = lax.broadcasted_iota(jnp.int32, (1, LANES), 1)
    cq_t = scale * (ck + (lane < MLA_NOPE).astype(F32))
    sq_t = scale * sk
    ones_col = (lane == MLA_V).astype(F32)
    for h in range(MLA_HEADS):
        qh = q[:, h * LANES:(h + 1) * LANES]
        q_out[h] = (qh * cq_t + pltpu.roll(qh, shift, axis=1) * sq_t).astype(BF16)
        k_out[h] = (kv[:, 2 * h * LANES:(2 * h + 1) * LANES] + kr).astype(BF16)
        v_out[h] = (kv[:, (2 * h + 1) * LANES:(2 * h + 2) * LANES] + ones_col).astype(BF16)


def _attn_kernel(q_ref, k_ref, v_ref, o_ref, m_ref, acc_ref):
    qi = pl.program_id(2)
    tq, tk = ATT_Q_BLOCK, ATT_KV_BLOCK
    per_q = tq // tk
    nt = (((1,), (1,)), ((), ()))
    m_ref[...] = jnp.full(m_ref.shape, -jnp.inf, F32)
    acc_ref[...] = jnp.zeros(acc_ref.shape, F32)

    def step(j, diag):
        start = pl.multiple_of(j * tk, tk)
        for hh in range(2):
            k = k_ref[hh, pl.ds(start, tk), :]
            v = v_ref[hh, pl.ds(start, tk), :]
            s = lax.dot_general(q_ref[hh], k, nt, preferred_element_type=F32)
            if diag is not None:
                ri = lax.broadcasted_iota(jnp.int32, (tq, tk), 0)
                ci = lax.broadcasted_iota(jnp.int32, (tq, tk), 1) + diag * tk
                s = jnp.where(ci <= ri, s, -jnp.inf)
            m_old = m_ref[hh]
            m_new = jnp.maximum(m_old, jnp.max(s, axis=1, keepdims=True))
            pexp = jnp.exp(s - m_new)
            acc_ref[hh] = acc_ref[hh] * jnp.exp(m_old - m_new) + jnp.dot(
                pexp.astype(BF16), v, preferred_element_type=F32)
            m_ref[hh] = m_new

    def body(j, carry):
        step(j, None)
        return carry

    lax.fori_loop(0, qi * per_q, body, 0)
    for dg in range(per_q):
        step(qi * per_q + dg, dg)
    outs = []
    for hh in range(2):
        acc = acc_ref[hh]
        outs.append(acc[:, :MLA_V] / acc[:, MLA_V:MLA_V + 1])
    o_ref[...] = jnp.concatenate(outs, axis=1).astype(o_ref.dtype)


def _mla_branch(small, positions, batch, seq, q_norm, w_uq, kv_norm, w_ukv):
    t = small.shape[0]
    hh = MLA_HEADS
    inv_freq = 1.0 / (ROPE_THETA ** (jnp.arange(0, MLA_ROPE, 2, dtype=F32) / MLA_ROPE))
    ang = positions.astype(F32).reshape(t, 1) * inv_freq
    cos2 = jnp.concatenate([jnp.cos(ang), jnp.cos(ang)], axis=-1)
    sin2 = jnp.concatenate([jnp.sin(ang), jnp.sin(ang)], axis=-1)
    ck = jnp.zeros((t, LANES), F32).at[:, MLA_NOPE:MLA_NOPE + MLA_ROPE].set(cos2)
    sk = jnp.zeros((t, LANES), F32).at[:, MLA_NOPE:MLA_NOPE + MLA_ROPE].set(sin2)
    wq = w_uq.astype(F32).reshape(MLA_Q_RANK, hh, MLA_NOPE + MLA_ROPE)
    wq = jnp.concatenate([wq, _rot_cols(wq[..., MLA_NOPE:])], axis=-1).reshape(MLA_Q_RANK, hh * LANES)
    wkv = w_ukv.astype(F32).reshape(MLA_KV_RANK, hh, MLA_NOPE + MLA_V)
    zpad = jnp.zeros((MLA_KV_RANK, hh, LANES - MLA_NOPE), F32)
    wkv = jnp.concatenate([wkv[..., :MLA_NOPE], zpad, wkv[..., MLA_NOPE:], zpad], axis=-1)
    wkv = wkv.reshape(MLA_KV_RANK, hh * 2 * LANES)
    tm = ROW_BLOCK
    const = lambda shape: pl.BlockSpec(shape, lambda i: tuple(0 for _ in shape))
    head_out = pl.BlockSpec((hh, tm, LANES), lambda i: (0, i, 0))
    q, k, v = pl.pallas_call(
        _mla_prep_kernel,
        grid=(t // tm,),
        in_specs=[pl.BlockSpec((tm, MLA_Q_RANK), lambda i: (i, 0)),
                  pl.BlockSpec((tm, MLA_KV_RANK), lambda i: (i, MLA_Q_RANK // MLA_KV_RANK)),
                  pl.BlockSpec((tm, LANES), lambda i: (i, (MLA_Q_RANK + MLA_KV_RANK) // LANES)),
                  pl.BlockSpec((tm, LANES), lambda i: (i, 0)),
                  pl.BlockSpec((tm, LANES), lambda i: (i, 0)),
                  const((1, MLA_Q_RANK)), const((1, MLA_KV_RANK)),
                  const((MLA_Q_RANK, hh * LANES)), const((MLA_KV_RANK, hh * 2 * LANES))],
        out_specs=[head_out, head_out, head_out],
        out_shape=[jax.ShapeDtypeStruct((hh, t, LANES), BF16)] * 3,
        compiler_params=_cparams(("parallel",)),
        name="mla_prep",
    )(small, small, small, ck, sk, q_norm.astype(F32).reshape(1, -1), kv_norm.astype(F32).reshape(1, -1),
      wq.astype(BF16), wkv.astype(BF16))
    blk = ATT_Q_BLOCK
    nq = seq // blk
    return pl.pallas_call(
        _attn_kernel,
        grid=(batch, hh // 2, nq),
        in_specs=[pl.BlockSpec((2, blk, LANES), lambda b, hp, i: (hp, b * nq + i, 0)),
                  pl.BlockSpec((2, seq, LANES), lambda b, hp, i: (hp, b, 0)),
                  pl.BlockSpec((2, seq, LANES), lambda b, hp, i: (hp, b, 0))],
        out_specs=pl.BlockSpec((blk, LANES), lambda b, hp, i: (b * nq + i, hp)),
        out_shape=jax.ShapeDtypeStruct((t, MLA_WIDTH), BF16),
        scratch_shapes=[pltpu.VMEM((2, blk, 1), F32), pltpu.VMEM((2, blk, LANES), F32)],
        compiler_params=_cparams(("parallel", "parallel", "arbitrary")),
        name="mla_attn",
    )(q, k, v)


def _merge_kernel(h_ref, ya_ref, yb_ref, yc_ref, g_ref, wa_ref, wb_ref, wc_ref, wo_ref, lg_ref, lb_ref,
                  wq_ref, h1_ref, qp_ref):
    d = D_MODEL
    gates = _sigmoid(g_ref[...].astype(F32))
    merged = (gates[:, :d] * jnp.dot(ya_ref[...], wa_ref[...], preferred_element_type=F32)
              + gates[:, d:2 * d] * jnp.dot(yb_ref[...], wb_ref[...], preferred_element_type=F32)
              + gates[:, 2 * d:] * jnp.dot(yc_ref[...], wc_ref[...], preferred_element_type=F32))
    mix = jnp.dot(merged.astype(BF16), wo_ref[...], preferred_element_type=F32)
    h1 = _ln_rows(DN_ALPHA * h_ref[...] + mix, lg_ref[...], lb_ref[...])
    h1_ref[...] = h1
    qp_ref[...] = jnp.dot(h1.astype(BF16), wq_ref[...], preferred_element_type=F32).astype(qp_ref.dtype)


def _merge(h, ya, yb, yc, gates, wa, wb, wc, wo, lg, lb, wq):
    t, d = h.shape
    tm = ROW_BLOCK
    row = lambda w: pl.BlockSpec((tm, w), lambda i: (i, 0))
    const = lambda a: pl.BlockSpec(a.shape, lambda i: (0, 0))
    ws = [wa.astype(BF16), wb.astype(BF16), wc.astype(BF16), wo.astype(BF16),
          lg.astype(F32).reshape(1, d), lb.astype(F32).reshape(1, d), wq.astype(BF16)]
    nq = wq.shape[1]
    return pl.pallas_call(
        _merge_kernel,
        grid=(t // tm,),
        in_specs=[row(d), row(ya.shape[1]), row(yb.shape[1]), row(yc.shape[1]), row(gates.shape[1])]
                 + [const(a) for a in ws],
        out_specs=[row(d), row(nq)],
        out_shape=[jax.ShapeDtypeStruct((t, d), F32), jax.ShapeDtypeStruct((t, nq), BF16)],
        compiler_params=_cparams(("parallel",)),
        name="merge",
    )(h, ya, yb, yc, gates, *ws)


_PEER_PAIRS = tuple((a, b) for a in range(PEER_TOPK) for b in range(PEER_TOPK) if (a + 1) * (b + 1) <= PEER_TOPK)


def _route_kernel(qp_ref, sk_ref, off_ref, par_ref, gate_ref, s_scr, i_scr):
    tt = PEER_ROUTE_BLOCK
    nk = PEER_NKEYS
    kd = PEER_KEY_DIM // 2
    nt = (((1,), (1,)), ((), ()))
    key_iota = lax.broadcasted_iota(jnp.int32, (nk, tt), 0)

    def head(h, carry):
        vals = []
        for j in range(2):
            q = qp_ref[:, pl.ds(pl.multiple_of((2 * h + j) * kd, kd), kd)]
            vals.append(lax.dot_general(sk_ref[j], q, nt, preferred_element_type=F32))
        for r in range(PEER_TOPK):
            for j in range(2):
                m = jnp.max(vals[j], axis=0, keepdims=True)
                i = jnp.min(jnp.where(vals[j] == m, key_iota, nk), axis=0, keepdims=True)
                vals[j] = jnp.where(key_iota == i, -jnp.inf, vals[j])
                s_scr[j * PEER_TOPK + r, pl.ds(h, 1), :] = m
                i_scr[j * PEER_TOPK + r, pl.ds(h, 1), :] = i
        return carry

    lax.fori_loop(0, PEER_HEADS, head, 0)

    s1 = [s_scr[r] for r in range(PEER_TOPK)]
    s2 = [s_scr[PEER_TOPK + r] for r in range(PEER_TOPK)]
    i1 = [i_scr[r] for r in range(PEER_TOPK)]
    i2 = [i_scr[PEER_TOPK + r] for r in range(PEER_TOPK)]
    cand = [s1[a] + s2[b] for a, b in _PEER_PAIRS]
    cidx = [i1[a] * nk + i2[b] for a, b in _PEER_PAIRS]
    best_s, best_i = [], []
    for r in range(PEER_TOPK):
        m = functools.reduce(jnp.maximum, cand)
        found = jnp.zeros(m.shape, jnp.bool_)
        sel = jnp.zeros(m.shape, jnp.int32)
        for c in range(len(cand)):
            hit = cand[c] == m
            take = jnp.logical_and(hit, jnp.logical_not(found))
            found = jnp.logical_or(found, hit)
            sel = jnp.where(take, cidx[c], sel)
            cand[c] = jnp.where(take, -jnp.inf, cand[c])
        best_s.append(m)
        best_i.append(sel)
    e = [jnp.exp(s - best_s[0]) for s in best_s]
    tot = functools.reduce(lambda p, q: p + q, e)
    sel_all = jnp.concatenate(best_i, axis=0)
    off_ref[...] = (sel_all >> 1).T
    par_ref[...] = (sel_all & 1).astype(F32).T
    gate_ref[...] = (jnp.concatenate(e, axis=0) / jnp.concatenate([tot] * PEER_TOPK, axis=0)).T


def _peer_route(qp, subkeys):
    t = qp.shape[0]
    tt = PEER_ROUTE_BLOCK
    out = pl.BlockSpec((tt, PEER_TOPK * PEER_HEADS), lambda i: (i, 0))
    shp = (t, PEER_TOPK * PEER_HEADS)
    off, par, gate = pl.pallas_call(
        _route_kernel,
        grid=(t // tt,),
        in_specs=[pl.BlockSpec((tt, qp.shape[1]), lambda i: (i, 0)),
                  pl.BlockSpec(subkeys.shape, lambda i: (0, 0, 0))],
        out_specs=[out, out, out],
        out_shape=[jax.ShapeDtypeStruct(shp, jnp.int32), jax.ShapeDtypeStruct(shp, F32),
                   jax.ShapeDtypeStruct(shp, F32)],
        scratch_shapes=[pltpu.VMEM((2 * PEER_TOPK, PEER_HEADS, tt), F32),
                        pltpu.VMEM((2 * PEER_TOPK, PEER_HEADS, tt), jnp.int32)],
        compiler_params=_cparams(("parallel",)),
        name="peer_route",
    )(qp, subkeys.astype(BF16))
    return off.reshape(-1), par, gate


NSEL = PEER_HEADS * PEER_TOPK
WROWS = NSEL * BF16_ROWS


def _peer_tables(tab):
    e, d = tab.shape
    c = d // LANES
    return tab.astype(BF16).reshape(e // 2, 2, c, LANES).transpose(0, 2, 1, 3).reshape(e // 2, 2 * c, LANES)


def _expand_consts():
    k = np.arange(WROWS) // BF16_ROWS
    r = np.arange(WROWS) % BF16_ROWS
    e = (k[None, :] == np.arange(NSEL)[:, None]).astype(np.float32)
    e2 = np.concatenate([e * (r % 2 == 0)[None, :], e * (r % 2 == 1)[None, :]], axis=0)
    return e, e2


GATHER_CHUNK = 16


def _gather_chunks(off_ref, tab_ref, t):
    base = t * NSEL
    return [jnp.concatenate([tab_ref[off_ref[base + c * GATHER_CHUNK + k]] for k in range(GATHER_CHUNK)], axis=0)
            for c in range(NSEL // GATHER_CHUNK)]


def _peer_up_kernel(off_ref, x_ref, par_ref, gate_ref, tab_ref, e_ref, et_ref, act_ref, w_scr, pe_scr, zz_scr):
    tt = PEER_BLOCK
    nt = (((1,), (1,)), ((), ()))
    par = par_ref[...]
    pe_scr[...] = jnp.dot(par.astype(BF16), e_ref[...], preferred_element_type=F32)
    row_r = lax.broadcasted_iota(jnp.int32, (SUBLANES, WROWS), 1) % BF16_ROWS
    chunk = lax.broadcasted_iota(jnp.int32, (SUBLANES, WROWS), 0)
    want_even = (row_r - 2 * chunk).astype(F32)

    def tok(t, carry):
        xt = x_ref[t].astype(BF16)
        r = jnp.concatenate([lax.dot_general(xt, w, nt, preferred_element_type=F32)
                             for w in _gather_chunks(off_ref, tab_ref, t)], axis=1)
        mask = want_even == pe_scr[pl.ds(t, 1), :]
        zz_scr[pl.ds(pl.multiple_of(t * SUBLANES, SUBLANES), SUBLANES), :] = jnp.where(mask, r, 0.0)
        return carry

    lax.fori_loop(0, tt, tok, 0, unroll=8)
    hd8 = jnp.dot(zz_scr[...].astype(BF16), et_ref[...], preferred_element_type=F32)
    hidden = jnp.sum(hd8.reshape(tt, SUBLANES, NSEL), axis=1)
    act = gate_ref[...] * _gelu_tanh(hidden)
    act_ref[:, :NSEL] = act * (1.0 - par)
    act_ref[:, NSEL:] = act * par


def _peer_down_kernel(off_ref, act_ref, h_ref, tab_ref, e2_ref, lg_ref, lb_ref, o_ref, w_scr, ae_scr, f_scr):
    tt = PEER_BLOCK
    ae_scr[...] = jnp.dot(act_ref[...].astype(BF16), e2_ref[...], preferred_element_type=F32)
    row_r = lax.broadcasted_iota(jnp.int32, (SUBLANES, WROWS), 1) % BF16_ROWS
    chunk = lax.broadcasted_iota(jnp.int32, (SUBLANES, WROWS), 0)
    m_chunk = (row_r >> 1) == chunk

    def tok(t, carry):
        lhs = jnp.where(m_chunk, ae_scr[pl.ds(t, 1), :], 0.0).astype(BF16)
        kc = GATHER_CHUNK * BF16_ROWS
        parts = [jnp.dot(lhs[:, c * kc:(c + 1) * kc], w, preferred_element_type=F32)
                 for c, w in enumerate(_gather_chunks(off_ref, tab_ref, t))]
        f_scr[t] = functools.reduce(lambda a, b: a + b, parts)
        return carry

    lax.fori_loop(0, tt, tok, 0, unroll=8)
    v = DN_ALPHA * h_ref[...] + f_scr[...]
    inv_d = 1.0 / D_MODEL
    mu = jnp.sum(jnp.sum(v, axis=2, keepdims=True), axis=1, keepdims=True) * inv_d
    vc = v - mu
    var = jnp.sum(jnp.sum(vc * vc, axis=2, keepdims=True), axis=1, keepdims=True) * inv_d
    o_ref[...] = vc * lax.rsqrt(var + LN_EPS) * lg_ref[...] + lb_ref[...]


def _peer(h1, qp, subkeys, u_tab, v_tab, lg, lb):
    t, d = h1.shape
    tt = PEER_BLOCK
    c = d // LANES
    off, par, gate = _peer_route(qp, subkeys)
    e, e2 = _expand_consts()
    e_bf, et_bf, e2_bf = jnp.asarray(e, BF16), jnp.asarray(e.T, BF16), jnp.asarray(e2, BF16)
    utab, vtab = _peer_tables(u_tab), _peer_tables(v_tab)
    h3 = h1.reshape(t, c, LANES)
    smem = pl.BlockSpec((tt * NSEL,), lambda i: (i,), memory_space=pltpu.SMEM)
    resident = lambda a: pl.BlockSpec(a.shape, lambda i: tuple(0 for _ in a.shape), pipeline_mode=pl.Buffered(1))
    rows = lambda w: pl.BlockSpec((tt, w), lambda i: (i, 0))
    tiles = pl.BlockSpec((tt, c, LANES), lambda i: (i, 0, 0))
    act = pl.pallas_call(
        _peer_up_kernel,
        grid=(t // tt,),
        in_specs=[smem, tiles, rows(NSEL), rows(NSEL), resident(utab), resident(e_bf), resident(et_bf)],
        out_specs=rows(2 * NSEL),
        out_shape=jax.ShapeDtypeStruct((t, 2 * NSEL), F32),
        scratch_shapes=[pltpu.VMEM((WROWS, LANES), BF16), pltpu.VMEM((tt, WROWS), F32),
                        pltpu.VMEM((tt * SUBLANES, WROWS), F32)],
        compiler_params=_cparams(("parallel",)),
        name="peer_up",
    )(off, h3, par, gate, utab, e_bf, et_bf)
    lg3 = lg.astype(F32).reshape(1, c, LANES)
    lb3 = lb.astype(F32).reshape(1, c, LANES)
    out = pl.pallas_call(
        _peer_down_kernel,
        grid=(t // tt,),
        in_specs=[smem, rows(2 * NSEL), tiles, resident(vtab), resident(e2_bf), resident(lg3), resident(lb3)],
        out_specs=tiles,
        out_shape=jax.ShapeDtypeStruct((t, c, LANES), F32),
        scratch_shapes=[pltpu.VMEM((WROWS, LANES), BF16), pltpu.VMEM((tt, WROWS), F32),
                        pltpu.VMEM((tt, c, LANES), F32)],
        compiler_params=_cparams(("parallel",)),
        name="peer_down",
    )(off, act, h3, vtab, e2_bf, lg3, lb3)
    return out.reshape(t, d)


def kernel(x, positions, ln_in_g, ln_in_b, w_in, s5_a_re, s5_a_im, s5_log_dt, s5_b_re, s5_b_im, s5_c_re, s5_c_im, s5_d, s5_w_glu, m2_conv_w, m2_conv_b, m2_dt_bias, m2_a_log, m2_d, m2_norm_w, mla_q_norm, mla_w_uq, mla_kv_norm, mla_w_ukv, w_br_a, w_br_b, w_br_c, w_out, ln1_g, ln1_b, peer_w_q, peer_subkeys, peer_u, peer_v, ln2_g, ln2_b):
    b, s, d = x.shape
    t = b * s
    assert d == D_MODEL and s % ATT_Q_BLOCK == 0 and ATT_Q_BLOCK % ATT_KV_BLOCK == 0 and s % SSD_CHUNK == 0 and t % ROW_BLOCK == 0
    c_s5, c_z, c_xbc, c_dt, c_q, c_kv, c_kr, _ = np.cumsum(IN_SPLITS).tolist()
    h = _layer_norm_in(x.reshape(t, d).astype(F32), ln_in_g.astype(F32), ln_in_b.astype(F32))
    for l in range(DEPTH):
        w = w_in[l].astype(F32)
        w_kr = w[:, c_kv:c_kr]
        w_small = jnp.concatenate(
            [w[:, c_dt:c_kv], w[:, c_xbc:c_dt], jnp.zeros((d, LANES // 2 - M2_HEADS), F32), w_kr, _rot_cols(w_kr)],
            axis=1)
        u = _matmul(h, w[:, :c_s5].astype(BF16), F32, S5_WIDTH, "proj_s5")
        zx = _matmul(h, w[:, c_s5:c_xbc].astype(BF16), F32, (c_xbc - c_s5) // 2, "proj_ssd")
        small = _matmul(h, w_small.astype(BF16), F32, w_small.shape[1], "proj_small")
        gates = _matmul(h, w[:, c_kr:].astype(BF16), BF16, D_MODEL, "proj_gates")
        ya = _s5_branch(u, s, s5_a_re[l], s5_a_im[l], s5_log_dt[l], s5_b_re[l], s5_b_im[l],
                        s5_c_re[l], s5_c_im[l], s5_d[l], s5_w_glu[l])
        yb = _ssd_branch(zx, small, (MLA_Q_RANK + MLA_KV_RANK) // LANES, b, s, m2_conv_w[l], m2_conv_b[l],
                         m2_dt_bias[l], m2_a_log[l], m2_d[l], m2_norm_w[l])
        yc = _mla_branch(small, positions, b, s, mla_q_norm[l], mla_w_uq[l], mla_kv_norm[l], mla_w_ukv[l])
        h1, qp = _merge(h, ya, yb, yc, gates, w_br_a[l], w_br_b[l], w_br_c[l], w_out[l], ln1_g[l], ln1_b[l],
                        peer_w_q[l])
        h = _peer(h1, qp, peer_subkeys[l], peer_u[l], peer_v[l], ln2_g[l], ln2_b[l])
    return h.reshape(b, s, d).astype(x.dtype)
```
